```python
import jax, jax.numpy as jnp
from jax import lax
import numpy as np

D_MODEL = 2048
BATCH = 16
SEQ = 2048
DEPTH = 1
DEC_BATCH = 32
DEC_SEQ = 32
PAST_LEN = 2048

CHUNK = 64
N_META = 16
MIX_WIDTH = D_MODEL
POOL_WIDTH = MIX_WIDTH // 2
CONV_WIDTH = MIX_WIDTH - POOL_WIDTH
POOL_WINDOWS = (2, 4, 8, 16)
POOL_GROUPS = len(POOL_WINDOWS)
POOL_GC = POOL_WIDTH // POOL_GROUPS
POOL_HIST = max(POOL_WINDOWS) - 1
CONV_HEADS = 8
CONV_K = 3
CONV_HIST = CONV_K - 1
IN_WIDTH = POOL_WIDTH + 3 * CONV_WIDTH
D_FF = 4 * D_MODEL
EPS = 1e-6

kernel_name = "hymba_pool_shortconv_stream_step"


def _rmsnorm(x, g):
    xf = x.astype(jnp.float32)
    r = lax.rsqrt(jnp.mean(xf * xf, axis=-1, keepdims=True) + EPS)
    return (xf * r * g.astype(jnp.float32)).astype(x.dtype)


def _pool_mixer(u, hist, hist_valid, w_pool, scale):
    B, L, C = u.shape
    ext = jnp.concatenate([hist.astype(u.dtype), u], axis=1)
    extf = ext.astype(jnp.float32)
    cs = jnp.concatenate([jnp.zeros((B, 1, C), jnp.float32), jnp.cumsum(extf, axis=1)], axis=1)
    valid = jnp.concatenate([jnp.full((POOL_HIST,), hist_valid, jnp.float32), jnp.ones((L,), jnp.float32)])
    cv = jnp.concatenate([jnp.zeros((1,), jnp.float32), jnp.cumsum(valid)])
    H1 = POOL_HIST + 1
    parts = []
    for g, w in enumerate(POOL_WINDOWS):
        sl = slice(g * POOL_GC, (g + 1) * POOL_GC)
        s = cs[:, H1:H1 + L, sl] - cs[:, H1 - w:H1 - w + L, sl]
        n = cv[H1:H1 + L] - cv[H1 - w:H1 - w + L]
        parts.append(s / n[None, :, None])
    mean = jnp.concatenate(parts, axis=-1)
    d = (mean - u.astype(jnp.float32)).astype(u.dtype).reshape(B, L, POOL_GROUPS, POOL_GC)
    y = jnp.einsum('blgc,gcd->blgd', d, w_pool).reshape(B, L, C) * scale
    return y, ext[:, -POOL_HIST:]


def _short_conv_mixer(bg, cg, v, hist, conv_w):
    z = cg * v
    L = z.shape[1]
    ext = jnp.concatenate([hist.astype(z.dtype), z], axis=1)
    out = ext[:, 0:L] * conv_w[0]
    for k in range(1, CONV_K):
        out = out + ext[:, k:k + L] * conv_w[k]
    return bg * out, ext[:, -CONV_HIST:]


def _layer(x, pool_hist, conv_hist, hist_valid, norm1_g, w_in, w_pool, pool_scale,
           conv_w, w_out, norm2_g, w_up, w_down):
    hn = _rmsnorm(x, norm1_g)
    proj = jnp.einsum('bld,de->ble', hn, w_in)
    u = proj[..., :POOL_WIDTH]
    bg = proj[..., POOL_WIDTH:POOL_WIDTH + CONV_WIDTH]
    cg = proj[..., POOL_WIDTH + CONV_WIDTH:POOL_WIDTH + 2 * CONV_WIDTH]
    v = proj[..., POOL_WIDTH + 2 * CONV_WIDTH:]
    y_pool, new_pool = _pool_mixer(u, pool_hist, hist_valid, w_pool, pool_scale)
    y_conv, new_conv = _short_conv_mixer(bg, cg, v, conv_hist, conv_w)
    mix = jnp.concatenate([y_pool, y_conv], axis=-1)
    x = x + jnp.einsum('ble,ed->bld', mix, w_out)
    hn2 = _rmsnorm(x, norm2_g)
    a = jax.nn.relu(jnp.einsum('bld,df->blf', hn2, w_up))
    x = x + jnp.einsum('blf,fd->bld', a * a, w_down)
    return x, new_pool, new_conv


def setup_inputs(seed: int = 0) -> dict:
    key = jax.random.key(seed)
    ks = jax.random.split(key, 16)
    f32 = jnp.float32
    x_prompt = jax.random.normal(ks[0], (BATCH, SEQ, D_MODEL), f32)
    x_sample = jax.random.normal(ks[1], (DEC_BATCH, DEC_SEQ, D_MODEL), f32)
    cache_pool = jax.random.normal(ks[2], (DEPTH, DEC_BATCH, POOL_HIST, POOL_WIDTH), f32)
    cache_conv = jax.random.normal(ks[3], (DEPTH, DEC_BATCH, CONV_HIST, CONV_WIDTH), f32)
    meta_tokens = jax.random.normal(ks[4], (N_META, D_MODEL), f32)
    norm1_g = 1.0 + 0.05 * jax.random.normal(ks[5], (DEPTH, D_MODEL), f32)
    w_in = jax.random.normal(ks[6], (DEPTH, D_MODEL, IN_WIDTH), f32) * D_MODEL ** -0.5
    w_pool = jax.random.normal(ks[7], (DEPTH, POOL_GROUPS, POOL_GC, POOL_GC), f32) * POOL_GC ** -0.5
    pool_scale = 1.0 + 0.1 * jax.random.normal(ks[8], (DEPTH, POOL_WIDTH), f32)
    conv_w = jax.random.normal(ks[9], (DEPTH, CONV_K, CONV_WIDTH), f32) * CONV_K ** -0.5
    w_out = jax.random.normal(ks[10], (DEPTH, MIX_WIDTH, D_MODEL), f32) * MIX_WIDTH ** -0.5
    norm2_g = 1.0 + 0.05 * jax.random.normal(ks[11], (DEPTH, D_MODEL), f32)
    w_up = jax.random.normal(ks[12], (DEPTH, D_MODEL, D_FF), f32) * D_MODEL ** -0.5
    w_down = jax.random.normal(ks[13], (DEPTH, D_FF, D_MODEL), f32) * D_FF ** -0.5
    final_g = 1.0 + 0.05 * jax.random.normal(ks[14], (D_MODEL,), f32)
    return {"x_prompt": x_prompt, "x_sample": x_sample, "cache_pool": cache_pool,
            "cache_conv": cache_conv, "meta_tokens": meta_tokens, "norm1_g": norm1_g,
            "w_in": w_in, "w_pool": w_pool, "pool_scale": pool_scale, "conv_w": conv_w,
            "w_out": w_out, "norm2_g": norm2_g, "w_up": w_up, "w_down": w_down,
            "final_g": final_g}


def reference(x_prompt, x_sample, cache_pool, cache_conv, meta_tokens, norm1_g, w_in, w_pool,
              pool_scale, conv_w, w_out, norm2_g, w_up, w_down, final_g):
    bp = x_prompt.shape[0]
    meta = jnp.broadcast_to(meta_tokens.astype(x_prompt.dtype)[None], (bp, N_META, x_prompt.shape[-1]))
    hp = jnp.concatenate([meta, x_prompt], axis=1)
    hs = x_sample
    zp_pool = jnp.zeros((bp, POOL_HIST, POOL_WIDTH), x_prompt.dtype)
    zp_conv = jnp.zeros((bp, CONV_HIST, CONV_WIDTH), x_prompt.dtype)
    sp_pool, sp_conv, ss_pool, ss_conv = [], [], [], []
    for l in range(DEPTH):
        hp, p_pool, p_conv = _layer(hp, zp_pool, zp_conv, 0.0, norm1_g[l], w_in[l], w_pool[l],
                                    pool_scale[l], conv_w[l], w_out[l], norm2_g[l], w_up[l], w_down[l])
        hs, s_pool, s_conv = _layer(hs, cache_pool[l], cache_conv[l], 1.0, norm1_g[l], w_in[l], w_pool[l],
                                    pool_scale[l], conv_w[l], w_out[l], norm2_g[l], w_up[l], w_down[l])
        sp_pool.append(p_pool)
        sp_conv.append(p_conv)
        ss_pool.append(s_pool)
        ss_conv.append(s_conv)
    y_prompt = _rmsnorm(hp, final_g)[:, N_META:]
    y_sample = _rmsnorm(hs, final_g)
    state_pool_prompt = jnp.stack(sp_pool, axis=0)
    state_conv_prompt = jnp.stack(sp_conv, axis=0)
    state_pool_sample = jnp.stack(ss_pool, axis=0)
    state_conv_sample = jnp.stack(ss_conv, axis=0)
    return (y_prompt, y_sample, state_pool_prompt, state_conv_prompt, state_pool_sample, state_conv_sample)
```

```python
import functools

import jax
import jax.numpy as jnp
from jax import lax
from jax.experimental import pallas as pl
from jax.experimental.pallas import tpu as pltpu

POOL_WINDOWS = (2, 4, 8, 16)
POOL_HIST = max(POOL_WINDOWS) - 1
CONV_K = 3
CONV_HIST = CONV_K - 1
EPS = 1e-6

SUBLANES = 8
POOL_HEAD = 16
CONV_HEAD = 8
assert POOL_HEAD % SUBLANES == 0 and POOL_HEAD >= POOL_HIST
assert CONV_HEAD % SUBLANES == 0 and CONV_HEAD >= CONV_HIST

MIB = 1024 * 1024


def _rmsnorm(x, g):
    r = lax.rsqrt(jnp.mean(x * x, axis=-1, keepdims=True) + EPS)
    return x * r * g


def _dot(a, b):
    return jnp.dot(a, b, preferred_element_type=jnp.float32)


def _mixer_kernel(x_ref, hp_ref, hc_ref, g1_ref, win_ref, wpool_ref, pscale_ref, convw_ref,
                  wout_ref, h_ref, sp_ref, sc_ref, extp_ref, extc_ref, mix_ref):
    S, L, D = x_ref.shape
    P = extp_ref.shape[-1]
    C = extc_ref.shape[-1]
    M = S * L
    gc = P // len(POOL_WINDOWS)
    j = pl.program_id(1)
    bf16 = jnp.bfloat16

    @pl.when(j == 0)
    def _():
        extp_ref[:, POOL_HEAD - POOL_HIST:POOL_HEAD, :] = hp_ref[...]
        extc_ref[:, CONV_HEAD - CONV_HIST:CONV_HEAD, :] = hc_ref[...]

    @pl.when(j > 0)
    def _():
        extp_ref[:, POOL_HEAD - POOL_HIST:POOL_HEAD, :] = extp_ref[:, L + POOL_HEAD - POOL_HIST:L + POOL_HEAD, :]
        extc_ref[:, CONV_HEAD - CONV_HIST:CONV_HEAD, :] = extc_ref[:, L + CONV_HEAD - CONV_HIST:L + CONV_HEAD, :]

    x = x_ref[...].reshape(M, D)
    hn = _rmsnorm(x, g1_ref[...]).astype(bf16)

    u = _dot(hn, win_ref[:, 0:P])
    extp_ref[:, POOL_HEAD:POOL_HEAD + L, :] = u.reshape(S, L, P)
    for g, w in enumerate(POOL_WINDOWS):
        cols = slice(g * gc, (g + 1) * gc)
        ug = extp_ref[:, POOL_HEAD:POOL_HEAD + L, cols]
        s = ug
        for k in range(1, w):
            s = s + extp_ref[:, POOL_HEAD - k:POOL_HEAD - k + L, cols]
        d = (s * (1.0 / w) - ug).reshape(M, gc).astype(bf16)
        y = _dot(d, wpool_ref[g]) * pscale_ref[:, cols]
        mix_ref[:, cols] = y.astype(bf16)

    cg = _dot(hn, win_ref[:, P + C:P + 2 * C])
    v = _dot(hn, win_ref[:, P + 2 * C:P + 3 * C])
    z = (cg * v).reshape(S, L, C)
    extc_ref[:, CONV_HEAD:CONV_HEAD + L, :] = z
    conv = extc_ref[:, CONV_HEAD - 2:CONV_HEAD - 2 + L, :] * convw_ref[0]
    conv = conv + extc_ref[:, CONV_HEAD - 1:CONV_HEAD - 1 + L, :] * convw_ref[1]
    conv = conv + z * convw_ref[2]
    bg = _dot(hn, win_ref[:, P:P + C])
    mix_ref[:, P:P + C] = (bg * conv.reshape(M, C)).astype(bf16)

    h = x + _dot(mix_ref[...], wout_ref[...])
    h_ref[...] = h.reshape(S, L, D)

    @pl.when(j == pl.num_programs(1) - 1)
    def _():
        sp_ref[...] = extp_ref[:, L + POOL_HEAD - POOL_HIST:L + POOL_HEAD, :]
        sc_ref[...] = extc_ref[:, L + CONV_HEAD - CONV_HIST:L + CONV_HEAD, :]


def _mixer_call(x, hist_pool, hist_conv, g1, w_in, w_pool, pool_scale, conv_w, w_out, *,
                seqs_per_block, rows_per_tile):
    B, T, D = x.shape
    P = hist_pool.shape[-1]
    C = hist_conv.shape[-1]
    S, L = seqs_per_block, rows_per_tile
    assert B % S == 0 and T % L == 0 and L % SUBLANES == 0 and L >= POOL_HIST
    shared_hist = hist_pool.shape[0] == 1
    assert shared_hist or hist_pool.shape[0] == B
    assert not shared_hist or S == 1

    def hist_map(b, j):
        return (0, 0, 0) if shared_hist else (b, 0, 0)

    def resident(a):
        return pl.BlockSpec(a.shape, lambda b, j: (0,) * a.ndim, pipeline_mode=pl.Buffered(1))

    weights = (g1, w_in, w_pool, pool_scale, conv_w, w_out)
    weight_bytes = sum(a.size * a.dtype.itemsize for a in weights)
    tile_bytes = S * L * D * 4
    vmem = weight_bytes + 4 * tile_bytes + 3 * S * (L + POOL_HEAD) * P * 4 + 5 * tile_bytes + 2 * MIB

    return pl.pallas_call(
        _mixer_kernel,
        grid=(B // S, T // L),
        in_specs=[
            pl.BlockSpec((S, L, D), lambda b, j: (b, j, 0)),
            pl.BlockSpec((S, POOL_HIST, P), hist_map),
            pl.BlockSpec((S, CONV_HIST, C), hist_map),
        ] + [resident(a) for a in weights],
        out_specs=[
            pl.BlockSpec((S, L, D), lambda b, j: (b, j, 0)),
            pl.BlockSpec((S, POOL_HIST, P), lambda b, j: (b, 0, 0)),
            pl.BlockSpec((S, CONV_HIST, C), lambda b, j: (b, 0, 0)),
        ],
        out_shape=[
            jax.ShapeDtypeStruct((B, T, D), x.dtype),
            jax.ShapeDtypeStruct((B, POOL_HIST, P), x.dtype),
            jax.ShapeDtypeStruct((B, CONV_HIST, C), x.dtype),
        ],
        scratch_shapes=[
            pltpu.VMEM((S, POOL_HEAD + L, P), jnp.float32),
            pltpu.VMEM((S, CONV_HEAD + L, C), jnp.float32),
            pltpu.VMEM((S * L, P + C), jnp.bfloat16),
        ],
        compiler_params=pltpu.CompilerParams(
            dimension_semantics=("arbitrary", "arbitrary"), vmem_limit_bytes=int(vmem)),
        name="mixer",
    )(x, hist_pool, hist_conv, *weights)


def _mlp_kernel(h_ref, g2_ref, wup_ref, wdown_ref, gf_ref, o_ref, hn_ref):
    j = pl.program_id(1)

    @pl.when(j == 0)
    def _():
        hn_ref[...] = _rmsnorm(h_ref[...], g2_ref[...]).astype(hn_ref.dtype)

    a = jnp.maximum(_dot(hn_ref[...], wup_ref[...]), 0.0)
    c = _dot((a * a).astype(wdown_ref.dtype), wdown_ref[...])

    @pl.when(j == 0)
    def _():
        o_ref[...] = h_ref[...] + c

    @pl.when(j > 0)
    def _():
        o_ref[...] += c

    @pl.when(j == pl.num_programs(1) - 1)
    def _():
        o_ref[...] = _rmsnorm(o_ref[...], gf_ref[...])


def _mlp_call(h, g2, w_up, w_down, gf, *, rows_per_tile, ff_chunk):
    N, D = h.shape
    F = w_up.shape[1]
    TM, FC = rows_per_tile, ff_chunk
    assert N % TM == 0 and F % FC == 0
    tile_bytes = TM * D * 4
    chunk_bytes = D * FC * w_up.dtype.itemsize
    vmem = 4 * tile_bytes + 4 * chunk_bytes + tile_bytes // 2 + 2 * TM * FC * 4 + 2 * tile_bytes + 2 * MIB

    return pl.pallas_call(
        _mlp_kernel,
        grid=(N // TM, F // FC),
        in_specs=[
            pl.BlockSpec((TM, D), lambda i, j: (i, 0)),
            pl.BlockSpec((1, D), lambda i, j: (0, 0)),
            pl.BlockSpec((D, FC), lambda i, j: (0, j)),
            pl.BlockSpec((FC, D), lambda i, j: (j, 0)),
            pl.BlockSpec((1, D), lambda i, j: (0, 0)),
        ],
        out_specs=pl.BlockSpec((TM, D), lambda i, j: (i, 0)),
        out_shape=jax.ShapeDtypeStruct((N, D), h.dtype),
        scratch_shapes=[pltpu.VMEM((TM, D), jnp.bfloat16)],
        compiler_params=pltpu.CompilerParams(
            dimension_semantics=("arbitrary", "arbitrary"), vmem_limit_bytes=int(vmem)),
        name="mlp",
    )(h, g2, w_up, w_down, gf)


def kernel(x_prompt, x_sample, cache_pool, cache_conv, meta_tokens, norm1_g, w_in, w_pool,
           pool_scale, conv_w, w_out, norm2_g, w_up, w_down, final_g):
    depth = norm1_g.shape[0]
    assert depth == 1, "history hand-off from the meta tokens is written for a single layer"
    bp, seq, d_model = x_prompt.shape
    bs, dec_seq, _ = x_sample.shape
    n_meta = meta_tokens.shape[0]
    pool_width = cache_pool.shape[-1]
    conv_width = cache_conv.shape[-1]
    assert n_meta >= POOL_HIST, "prompt rows must see only full pooling windows"
    bf16 = jnp.bfloat16
    f32 = jnp.float32

    mixer_weights = (norm1_g[0][None], w_in[0].astype(bf16), w_pool[0].astype(bf16),
                     pool_scale[0][None], conv_w[0], w_out[0].astype(bf16))
    mlp_weights = (norm2_g[0][None], w_up[0].astype(bf16), w_down[0].astype(bf16), final_g[None])

    _, meta_pool, meta_conv = _mixer_call(
        meta_tokens.astype(x_prompt.dtype)[None],
        jnp.zeros((1, POOL_HIST, pool_width), f32), jnp.zeros((1, CONV_HIST, conv_width), f32),
        *mixer_weights, seqs_per_block=1, rows_per_tile=n_meta)

    hp, sp_pool, sp_conv = _mixer_call(
        x_prompt, meta_pool, meta_conv, *mixer_weights, seqs_per_block=1, rows_per_tile=256)
    hs, ss_pool, ss_conv = _mixer_call(
        x_sample, cache_pool[0], cache_conv[0], *mixer_weights,
        seqs_per_block=8, rows_per_tile=dec_seq)

    y_prompt = _mlp_call(hp.reshape(bp * seq, d_model), *mlp_weights,
                         rows_per_tile=512, ff_chunk=1024).reshape(bp, seq, d_model)
    y_sample = _mlp_call(hs.reshape(bs * dec_seq, d_model), *mlp_weights,
                         rows_per_tile=512, ff_chunk=1024).reshape(bs, dec_seq, d_model)

    return (y_prompt, y_sample, sp_pool[None], sp_conv[None], ss_pool[None], ss_conv[None])
```

```python
import functools

import jax
import jax.numpy as jnp
from jax import lax
from jax.experimental import pallas as pl
from jax.experimental.pallas import tpu as pltpu

POOL_WINDOWS = (2, 4, 8, 16)
POOL_HIST = max(POOL_WINDOWS) - 1
CONV_K = 3
CONV_HIST = CONV_K - 1
EPS = 1e-6

SUBLANES = 8
POOL_HEAD = 16
CONV_HEAD = 8
assert POOL_HEAD % SUBLANES == 0 and POOL_HEAD >= POOL_HIST
assert CONV_HEAD % SUBLANES == 0 and CONV_HEAD >= CONV_HIST

MIB = 1024 * 1024

MIXER_ROWS = 512
MLP_ROWS = 512
MLP_FF_CHUNK = 2048
MLP_SLAB = 512


def _rmsnorm(x, g):
    r = lax.rsqrt(jnp.mean(x * x, axis=-1, keepdims=True) + EPS)
    return x * r * g


def _dot(a, b):
    return jnp.dot(a, b, preferred_element_type=jnp.float32)


def _mixer_kernel(x_ref, hp_ref, hc_ref, g1_ref, win_ref, wpool_ref, pscale_ref, convw_ref,
                  wout_ref, h_ref, sp_ref, sc_ref, extp_ref, extc_ref, mix_ref):
    S, L, D = x_ref.shape
    P = extp_ref.shape[-1]
    C = extc_ref.shape[-1]
    M = S * L
    gc = P // len(POOL_WINDOWS)
    j = pl.program_id(1)
    bf16 = jnp.bfloat16

    @pl.when(j == 0)
    def _():
        extp_ref[:, POOL_HEAD - POOL_HIST:POOL_HEAD, :] = hp_ref[...]
        extc_ref[:, CONV_HEAD - CONV_HIST:CONV_HEAD, :] = hc_ref[...]

    @pl.when(j > 0)
    def _():
        extp_ref[:, POOL_HEAD - POOL_HIST:POOL_HEAD, :] = extp_ref[:, L + POOL_HEAD - POOL_HIST:L + POOL_HEAD, :]
        extc_ref[:, CONV_HEAD - CONV_HIST:CONV_HEAD, :] = extc_ref[:, L + CONV_HEAD - CONV_HIST:L + CONV_HEAD, :]

    x = x_ref[...].reshape(M, D)
    hn = _rmsnorm(x, g1_ref[...]).astype(bf16)

    u = _dot(hn, win_ref[:, 0:P])
    extp_ref[:, POOL_HEAD:POOL_HEAD + L, :] = u.reshape(S, L, P)
    for g, w in enumerate(POOL_WINDOWS):
        cols = slice(g * gc, (g + 1) * gc)
        ug = extp_ref[:, POOL_HEAD:POOL_HEAD + L, cols]
        s = ug
        for k in range(1, w):
            s = s + extp_ref[:, POOL_HEAD - k:POOL_HEAD - k + L, cols]
        d = (s * (1.0 / w) - ug).reshape(M, gc).astype(bf16)
        y = _dot(d, wpool_ref[g]) * pscale_ref[:, cols]
        mix_ref[:, cols] = y.astype(bf16)

    cg = _dot(hn, win_ref[:, P + C:P + 2 * C])
    v = _dot(hn, win_ref[:, P + 2 * C:P + 3 * C])
    z = (cg * v).reshape(S, L, C)
    extc_ref[:, CONV_HEAD:CONV_HEAD + L, :] = z
    conv = extc_ref[:, CONV_HEAD - 2:CONV_HEAD - 2 + L, :] * convw_ref[0]
    conv = conv + extc_ref[:, CONV_HEAD - 1:CONV_HEAD - 1 + L, :] * convw_ref[1]
    conv = conv + z * convw_ref[2]
    bg = _dot(hn, win_ref[:, P:P + C])
    mix_ref[:, P:P + C] = (bg * conv.reshape(M, C)).astype(bf16)

    h = x + _dot(mix_ref[...], wout_ref[...])
    h_ref[...] = h.reshape(S, L, D)

    @pl.when(j == pl.num_programs(1) - 1)
    def _():
        sp_ref[...] = extp_ref[:, L + POOL_HEAD - POOL_HIST:L + POOL_HEAD, :]
        sc_ref[...] = extc_ref[:, L + CONV_HEAD - CONV_HIST:L + CONV_HEAD, :]


def _mixer_call(x, hist_pool, hist_conv, g1, w_in, w_pool, pool_scale, conv_w, w_out, *,
                seqs_per_block, rows_per_tile):
    B, T, D = x.shape
    P = hist_pool.shape[-1]
    C = hist_conv.shape[-1]
    S, L = seqs_per_block, rows_per_tile
    assert B % S == 0 and T % L == 0 and L % SUBLANES == 0 and L >= POOL_HIST
    shared_hist = hist_pool.shape[0] == 1
    assert shared_hist or hist_pool.shape[0] == B
    assert not shared_hist or S == 1

    def hist_map(b, j):
        return (0, 0, 0) if shared_hist else (b, 0, 0)

    def resident(a):
        return pl.BlockSpec(a.shape, lambda b, j: (0,) * a.ndim, pipeline_mode=pl.Buffered(1))

    weights = (g1, w_in, w_pool, pool_scale, conv_w, w_out)
    weight_bytes = sum(a.size * a.dtype.itemsize for a in weights)
    tile_bytes = S * L * D * 4
    scratch_bytes = S * ((L + POOL_HEAD) * P + (L + CONV_HEAD) * C) * 4 + S * L * (P + C) * 2
    vmem = weight_bytes + 4 * tile_bytes + scratch_bytes + 2 * tile_bytes + 2 * MIB

    return pl.pallas_call(
        _mixer_kernel,
        grid=(B // S, T // L),
        in_specs=[
            pl.BlockSpec((S, L, D), lambda b, j: (b, j, 0)),
            pl.BlockSpec((S, POOL_HIST, P), hist_map),
            pl.BlockSpec((S, CONV_HIST, C), hist_map),
        ] + [resident(a) for a in weights],
        out_specs=[
            pl.BlockSpec((S, L, D), lambda b, j: (b, j, 0)),
            pl.BlockSpec((S, POOL_HIST, P), lambda b, j: (b, 0, 0)),
            pl.BlockSpec((S, CONV_HIST, C), lambda b, j: (b, 0, 0)),
        ],
        out_shape=[
            jax.ShapeDtypeStruct((B, T, D), x.dtype),
            jax.ShapeDtypeStruct((B, POOL_HIST, P), x.dtype),
            jax.ShapeDtypeStruct((B, CONV_HIST, C), x.dtype),
        ],
        scratch_shapes=[
            pltpu.VMEM((S, POOL_HEAD + L, P), jnp.float32),
            pltpu.VMEM((S, CONV_HEAD + L, C), jnp.float32),
            pltpu.VMEM((S * L, P + C), jnp.bfloat16),
        ],
        compiler_params=pltpu.CompilerParams(
            dimension_semantics=("arbitrary", "arbitrary"), vmem_limit_bytes=int(vmem)),
        name="mixer",
    )(x, hist_pool, hist_conv, *weights)


def _mlp_kernel(h_ref, g2_ref, wup_ref, wdown_ref, gf_ref, o_ref, hn_ref, a_ref):
    j = pl.program_id(1)

    @pl.when(j == 0)
    def _():
        h = h_ref[...]
        hn_ref[...] = _rmsnorm(h, g2_ref[...]).astype(hn_ref.dtype)
        o_ref[...] = h

    fc = wup_ref.shape[1]
    for k in range(0, fc, MLP_SLAB):
        a = jnp.maximum(_dot(hn_ref[...], wup_ref[:, k:k + MLP_SLAB]), 0.0)
        a_ref[:, k:k + MLP_SLAB] = (a * a).astype(a_ref.dtype)
    o_ref[...] += _dot(a_ref[...], wdown_ref[...])

    @pl.when(j == pl.num_programs(1) - 1)
    def _():
        o_ref[...] = _rmsnorm(o_ref[...], gf_ref[...])


def _mlp_call(h, g2, w_up, w_down, gf, *, rows_per_tile, ff_chunk):
    N, D = h.shape
    F = w_up.shape[1]
    TM, FC = rows_per_tile, ff_chunk
    assert N % TM == 0 and F % FC == 0 and FC % MLP_SLAB == 0
    tile_bytes = TM * D * 4
    chunk_bytes = D * FC * w_up.dtype.itemsize
    vmem = (4 * tile_bytes + 4 * chunk_bytes + tile_bytes // 2 + TM * FC * 2
            + 3 * TM * MLP_SLAB * 4 + tile_bytes // 2 + 2 * MIB)

    return pl.pallas_call(
        _mlp_kernel,
        grid=(N // TM, F // FC),
        in_specs=[
            pl.BlockSpec((TM, D), lambda i, j: (i, 0)),
            pl.BlockSpec((1, D), lambda i, j: (0, 0)),
            pl.BlockSpec((D, FC), lambda i, j: (0, j)),
            pl.BlockSpec((FC, D), lambda i, j: (j, 0)),
            pl.BlockSpec((1, D), lambda i, j: (0, 0)),
        ],
        out_specs=pl.BlockSpec((TM, D), lambda i, j: (i, 0)),
        out_shape=jax.ShapeDtypeStruct((N, D), h.dtype),
        scratch_shapes=[pltpu.VMEM((TM, D), jnp.bfloat16), pltpu.VMEM((TM, FC), jnp.bfloat16)],
        compiler_params=pltpu.CompilerParams(
            dimension_semantics=("arbitrary", "arbitrary"), vmem_limit_bytes=int(vmem)),
        name="mlp",
    )(h, g2, w_up, w_down, gf)


def kernel(x_prompt, x_sample, cache_pool, cache_conv, meta_tokens, norm1_g, w_in, w_pool,
           pool_scale, conv_w, w_out, norm2_g, w_up, w_down, final_g):
    depth = norm1_g.shape[0]
    assert depth == 1, "history hand-off from the meta tokens is written for a single layer"
    bp, seq, d_model = x_prompt.shape
    bs, dec_seq, _ = x_sample.shape
    n_meta = meta_tokens.shape[0]
    pool_width = cache_pool.shape[-1]
    conv_width = cache_conv.shape[-1]
    assert n_meta >= POOL_HIST, "prompt rows must see only full pooling windows"
    bf16 = jnp.bfloat16
    f32 = jnp.float32

    mixer_weights = (norm1_g[0][None], w_in[0].astype(bf16), w_pool[0].astype(bf16),
                     pool_scale[0][None], conv_w[0], w_out[0].astype(bf16))
    mlp_weights = (norm2_g[0][None], w_up[0].astype(bf16), w_down[0].astype(bf16), final_g[None])

    _, meta_pool, meta_conv = _mixer_call(
        meta_tokens.astype(x_prompt.dtype)[None],
        jnp.zeros((1, POOL_HIST, pool_width), f32), jnp.zeros((1, CONV_HIST, conv_width), f32),
        *mixer_weights, seqs_per_block=1, rows_per_tile=n_meta)

    hp, sp_pool, sp_conv = _mixer_call(
        x_prompt, meta_pool, meta_conv, *mixer_weights,
        seqs_per_block=1, rows_per_tile=MIXER_ROWS)
    hs, ss_pool, ss_conv = _mixer_call(
        x_sample, cache_pool[0], cache_conv[0], *mixer_weights,
        seqs_per_block=MIXER_ROWS // dec_seq, rows_per_tile=dec_seq)

    y_prompt = _mlp_call(hp.reshape(bp * seq, d_model), *mlp_weights,
                         rows_per_tile=MLP_ROWS, ff_chunk=MLP_FF_CHUNK).reshape(bp, seq, d_model)
    y_sample = _mlp_call(hs.reshape(bs * dec_seq, d_model), *mlp_weights,
                         rows_per_tile=MLP_ROWS, ff_chunk=MLP_FF_CHUNK).reshape(bs, dec_seq, d_model)

    return (y_prompt, y_sample, sp_pool[None], sp_conv[None], ss_pool[None], ss_conv[None])
```

```python
import functools

import jax
import jax.numpy as jnp
from jax import lax
from jax.experimental import pallas as pl
from jax.experimental.pallas import tpu as pltpu

POOL_WINDOWS = (2, 4, 8, 16)
POOL_HIST = max(POOL_WINDOWS) - 1
CONV_K = 3
CONV_HIST = CONV_K - 1
EPS = 1e-6

SUBLANES = 8
POOL_HEAD = 16
CONV_HEAD = 8
assert POOL_HEAD % SUBLANES == 0 and POOL_HEAD >= POOL_HIST
assert CONV_HEAD % SUBLANES == 0 and CONV_HEAD >= CONV_HIST

MIB = 1024 * 1024

MATMUL_DTYPE = jnp.bfloat16

MIXER_ROWS = 512
MLP_ROWS = 512
MLP_FF_CHUNK = 2048
MLP_SLAB = 512
MLP_FF_CHUNK_ROUNDING = 512


def _rmsnorm(x, g):
    r = lax.rsqrt(jnp.mean(x * x, axis=-1, keepdims=True) + EPS)
    return x * r * g


def _dot(a, b):
    return jnp.dot(a, b, preferred_element_type=jnp.float32)


def _mixer_kernel(x_ref, hp_ref, hc_ref, g1_ref, win_ref, wpool_ref, pscale_ref, convw_ref,
                  wout_ref, h_ref, sp_ref, sc_ref, extp_ref, extc_ref, mix_ref):
    S, L, D = x_ref.shape
    P = extp_ref.shape[-1]
    C = extc_ref.shape[-1]
    M = S * L
    gc = P // len(POOL_WINDOWS)
    j = pl.program_id(1)
    bf16 = MATMUL_DTYPE

    @pl.when(j == 0)
    def _():
        extp_ref[:, POOL_HEAD - POOL_HIST:POOL_HEAD, :] = hp_ref[...]
        extc_ref[:, CONV_HEAD - CONV_HIST:CONV_HEAD, :] = hc_ref[...]

    @pl.when(j > 0)
    def _():
        extp_ref[:, POOL_HEAD - POOL_HIST:POOL_HEAD, :] = extp_ref[:, L + POOL_HEAD - POOL_HIST:L + POOL_HEAD, :]
        extc_ref[:, CONV_HEAD - CONV_HIST:CONV_HEAD, :] = extc_ref[:, L + CONV_HEAD - CONV_HIST:L + CONV_HEAD, :]

    x = x_ref[...].reshape(M, D)
    hn = _rmsnorm(x, g1_ref[...]).astype(bf16)

    u = _dot(hn, win_ref[:, 0:P])
    extp_ref[:, POOL_HEAD:POOL_HEAD + L, :] = u.reshape(S, L, P)

    cg = _dot(hn, win_ref[:, P + C:P + 2 * C])
    v = _dot(hn, win_ref[:, P + 2 * C:P + 3 * C])
    z = (cg * v).reshape(S, L, C)
    extc_ref[:, CONV_HEAD:CONV_HEAD + L, :] = z
    conv = extc_ref[:, CONV_HEAD - 2:CONV_HEAD - 2 + L, :] * convw_ref[0]
    conv = conv + extc_ref[:, CONV_HEAD - 1:CONV_HEAD - 1 + L, :] * convw_ref[1]
    conv = conv + z * convw_ref[2]
    bg = _dot(hn, win_ref[:, P:P + C])
    mix_ref[:, P:P + C] = (bg * conv.reshape(M, C)).astype(bf16)

    for g, w in enumerate(POOL_WINDOWS):
        cols = slice(g * gc, (g + 1) * gc)
        ug = extp_ref[:, POOL_HEAD:POOL_HEAD + L, cols]
        s = ug
        for k in range(1, w):
            s = s + extp_ref[:, POOL_HEAD - k:POOL_HEAD - k + L, cols]
        d = (s * (1.0 / w) - ug).reshape(M, gc).astype(bf16)
        y = _dot(d, wpool_ref[g]) * pscale_ref[:, cols]
        mix_ref[:, cols] = y.astype(bf16)

    h = x + _dot(mix_ref[...], wout_ref[...])
    h_ref[...] = h.reshape(S, L, D)

    @pl.when(j == pl.num_programs(1) - 1)
    def _():
        sp_ref[...] = extp_ref[:, L + POOL_HEAD - POOL_HIST:L + POOL_HEAD, :]
        sc_ref[...] = extc_ref[:, L + CONV_HEAD - CONV_HIST:L + CONV_HEAD, :]


def _mixer_call(x, hist_pool, hist_conv, g1, w_in, w_pool, pool_scale, conv_w, w_out, *,
                seqs_per_block, rows_per_tile):
    B, T, D = x.shape
    P = hist_pool.shape[-1]
    C = hist_conv.shape[-1]
    S, L = seqs_per_block, rows_per_tile
    assert B % S == 0 and T % L == 0 and L % SUBLANES == 0 and L >= POOL_HIST
    shared_hist = hist_pool.shape[0] == 1
    assert shared_hist or hist_pool.shape[0] == B
    assert not shared_hist or S == 1

    def hist_map(b, j):
        return (0, 0, 0) if shared_hist else (b, 0, 0)

    def resident(a):
        return pl.BlockSpec(a.shape, lambda b, j: (0,) * a.ndim, pipeline_mode=pl.Buffered(1))

    weights = (g1, w_in, w_pool, pool_scale, conv_w, w_out)
    weight_bytes = sum(a.size * a.dtype.itemsize for a in weights)
    tile_bytes = S * L * D * 4
    scratch_bytes = S * ((L + POOL_HEAD) * P + (L + CONV_HEAD) * C) * 4 + S * L * (P + C) * 2
    vmem = weight_bytes + 4 * tile_bytes + scratch_bytes + 2 * tile_bytes + 2 * MIB

    return pl.pallas_call(
        _mixer_kernel,
        grid=(B // S, T // L),
        in_specs=[
            pl.BlockSpec((S, L, D), lambda b, j: (b, j, 0)),
            pl.BlockSpec((S, POOL_HIST, P), hist_map),
            pl.BlockSpec((S, CONV_HIST, C), hist_map),
        ] + [resident(a) for a in weights],
        out_specs=[
            pl.BlockSpec((S, L, D), lambda b, j: (b, j, 0)),
            pl.BlockSpec((S, POOL_HIST, P), lambda b, j: (b, 0, 0)),
            pl.BlockSpec((S, CONV_HIST, C), lambda b, j: (b, 0, 0)),
        ],
        out_shape=[
            jax.ShapeDtypeStruct((B, T, D), x.dtype),
            jax.ShapeDtypeStruct((B, POOL_HIST, P), x.dtype),
            jax.ShapeDtypeStruct((B, CONV_HIST, C), x.dtype),
        ],
        scratch_shapes=[
            pltpu.VMEM((S, POOL_HEAD + L, P), jnp.float32),
            pltpu.VMEM((S, CONV_HEAD + L, C), jnp.float32),
            pltpu.VMEM((S * L, P + C), MATMUL_DTYPE),
        ],
        compiler_params=pltpu.CompilerParams(
            dimension_semantics=("arbitrary", "arbitrary"), vmem_limit_bytes=int(vmem)),
        name="mixer",
    )(x, hist_pool, hist_conv, *weights)


def _mlp_kernel(h_ref, g2_ref, wup_ref, wdown_ref, gf_ref, o_ref, *rest, emit_weights):
    hn_ref, a_ref = rest[-2:]
    j = pl.program_id(1)

    @pl.when(j == 0)
    def _():
        h = h_ref[...]
        hn_ref[...] = _rmsnorm(h, g2_ref[...]).astype(hn_ref.dtype)
        o_ref[...] = h

    if emit_weights:
        wup_out_ref, wdown_out_ref = rest[:2]
        wup_out_ref[...] = wup_ref[...].astype(wup_out_ref.dtype)
        wdown_out_ref[...] = wdown_ref[...].astype(wdown_out_ref.dtype)
        wup_ref, wdown_ref = wup_out_ref, wdown_out_ref

    fc = wup_ref.shape[1]
    slab = min(MLP_SLAB, fc)
    for k in range(0, fc, slab):
        a = jnp.maximum(_dot(hn_ref[...], wup_ref[:, k:k + slab]), 0.0)
        a_ref[:, k:k + slab] = (a * a).astype(a_ref.dtype)
    o_ref[...] += _dot(a_ref[...], wdown_ref[...])

    @pl.when(j == pl.num_programs(1) - 1)
    def _():
        o_ref[...] = _rmsnorm(o_ref[...], gf_ref[...])


def _mlp_call(h, g2, w_up, w_down, gf, *, rows_per_tile, ff_chunk):
    N, D = h.shape
    F = w_up.shape[1]
    TM, FC = rows_per_tile, ff_chunk
    slab = min(MLP_SLAB, FC)
    assert N % TM == 0 and F % FC == 0 and FC % slab == 0
    emit_weights = w_up.dtype != MATMUL_DTYPE
    assert w_down.dtype == w_up.dtype
    assert not emit_weights or N == TM
    single_tile = N == TM
    mm_bytes = jnp.dtype(MATMUL_DTYPE).itemsize
    tile_bytes = TM * D * 4
    chunk_elems = D * FC
    vmem = ((3 if single_tile else 4) * tile_bytes
            + 4 * chunk_elems * w_up.dtype.itemsize
            + (4 * chunk_elems * mm_bytes if emit_weights else 0)
            + TM * D * mm_bytes + TM * FC * mm_bytes + 3 * TM * slab * 4 + tile_bytes // 2 + 2 * MIB)

    h_spec = pl.BlockSpec((TM, D), lambda i, j: (i, 0),
                          pipeline_mode=pl.Buffered(1) if single_tile else None)
    y_spec = pl.BlockSpec((TM, D), lambda i, j: (i, 0))
    y_shape = jax.ShapeDtypeStruct((N, D), h.dtype)
    if emit_weights:
        out_specs = [y_spec, pl.BlockSpec((D, FC), lambda i, j: (0, j)),
                     pl.BlockSpec((FC, D), lambda i, j: (j, 0))]
        out_shape = [y_shape, jax.ShapeDtypeStruct(w_up.shape, MATMUL_DTYPE),
                     jax.ShapeDtypeStruct(w_down.shape, MATMUL_DTYPE)]
    else:
        out_specs, out_shape = y_spec, y_shape

    return pl.pallas_call(
        functools.partial(_mlp_kernel, emit_weights=emit_weights),
        grid=(N // TM, F // FC),
        in_specs=[
            h_spec,
            pl.BlockSpec((1, D), lambda i, j: (0, 0)),
            pl.BlockSpec((D, FC), lambda i, j: (0, j)),
            pl.BlockSpec((FC, D), lambda i, j: (j, 0)),
            pl.BlockSpec((1, D), lambda i, j: (0, 0)),
        ],
        out_specs=out_specs,
        out_shape=out_shape,
        scratch_shapes=[pltpu.VMEM((TM, D), MATMUL_DTYPE), pltpu.VMEM((TM, FC), MATMUL_DTYPE)],
        compiler_params=pltpu.CompilerParams(
            dimension_semantics=("arbitrary", "arbitrary"), vmem_limit_bytes=int(vmem)),
        name="mlp",
    )(h, g2, w_up, w_down, gf)


def kernel(x_prompt, x_sample, cache_pool, cache_conv, meta_tokens, norm1_g, w_in, w_pool,
           pool_scale, conv_w, w_out, norm2_g, w_up, w_down, final_g):
    depth = norm1_g.shape[0]
    assert depth == 1, "history hand-off from the meta tokens is written for a single layer"
    bp, seq, d_model = x_prompt.shape
    bs, dec_seq, _ = x_sample.shape
    n_meta = meta_tokens.shape[0]
    pool_width = cache_pool.shape[-1]
    conv_width = cache_conv.shape[-1]
    assert n_meta >= POOL_HIST, "prompt rows must see only full pooling windows"
    bf16 = MATMUL_DTYPE
    f32 = jnp.float32

    mixer_weights = (norm1_g[0][None], w_in[0].astype(bf16), w_pool[0].astype(bf16),
                     pool_scale[0][None], conv_w[0], w_out[0].astype(bf16))

    _, meta_pool, meta_conv = _mixer_call(
        meta_tokens.astype(x_prompt.dtype)[None],
        jnp.zeros((1, POOL_HIST, pool_width), f32), jnp.zeros((1, CONV_HIST, conv_width), f32),
        *mixer_weights, seqs_per_block=1, rows_per_tile=n_meta)

    hp, sp_pool, sp_conv = _mixer_call(
        x_prompt, meta_pool, meta_conv, *mixer_weights,
        seqs_per_block=1, rows_per_tile=MIXER_ROWS)
    hs, ss_pool, ss_conv = _mixer_call(
        x_sample, cache_pool[0], cache_conv[0], *mixer_weights,
        seqs_per_block=MIXER_ROWS // dec_seq, rows_per_tile=dec_seq)

    g2, gf = norm2_g[0][None], final_g[None]
    y_sample, w_up_mm, w_down_mm = _mlp_call(
        hs.reshape(bs * dec_seq, d_model), g2, w_up[0], w_down[0], gf,
        rows_per_tile=bs * dec_seq, ff_chunk=MLP_FF_CHUNK_ROUNDING)
    y_prompt = _mlp_call(hp.reshape(bp * seq, d_model), g2, w_up_mm, w_down_mm, gf,
                         rows_per_tile=MLP_ROWS, ff_chunk=MLP_FF_CHUNK)
    y_prompt = y_prompt.reshape(bp, seq, d_model)
    y_sample = y_sample.reshape(bs, dec_seq, d_model)

    return (y_prompt, y_sample, sp_pool[None], sp_conv[None], ss_pool[None], ss_conv[None])
```

```python
import functools

import jax
import jax.numpy as jnp
from jax import lax
from jax.experimental import pallas as pl
from jax.experimental.pallas import tpu as pltpu

POOL_WINDOWS = (2, 4, 8, 16)
POOL_HIST = max(POOL_WINDOWS) - 1
CONV_K = 3
CONV_HIST = CONV_K - 1
EPS = 1e-6

SUBLANES = 8
POOL_HEAD = 16
CONV_HEAD = 8
assert POOL_HEAD % SUBLANES == 0 and POOL_HEAD >= POOL_HIST
assert CONV_HEAD % SUBLANES == 0 and CONV_HEAD >= CONV_HIST

MIB = 1024 * 1024

MATMUL_DTYPE = jnp.bfloat16

MIXER_ROWS = 512
MLP_ROWS = 512
MLP_FF_CHUNK = 2048
MLP_SLAB = 512
MLP_FF_CHUNK_ROUNDING = 512
MLP_EDGE_PARTS = 2


def _rmsnorm(x, g):
    r = lax.rsqrt(jnp.mean(x * x, axis=-1, keepdims=True) + EPS)
    return x * r * g


def _dot(a, b):
    return jnp.dot(a, b, preferred_element_type=jnp.float32)


def _mixer_kernel(x_ref, hp_ref, hc_ref, g1_ref, win_ref, wpool_ref, pscale_ref, convw_ref,
                  wout_ref, h_ref, sp_ref, sc_ref, extp_ref, extc_ref, mix_ref):
    S, L, D = x_ref.shape
    P = extp_ref.shape[-1]
    C = extc_ref.shape[-1]
    gc = P // len(POOL_WINDOWS)
    j = pl.program_id(1)
    bf16 = MATMUL_DTYPE

    @pl.when(j == 0)
    def _():
        extp_ref[:, POOL_HEAD - POOL_HIST:POOL_HEAD, :] = hp_ref[...]
        extc_ref[:, CONV_HEAD - CONV_HIST:CONV_HEAD, :] = hc_ref[...]

    @pl.when(j > 0)
    def _():
        extp_ref[:, POOL_HEAD - POOL_HIST:POOL_HEAD, :] = extp_ref[:, L + POOL_HEAD - POOL_HIST:L + POOL_HEAD, :]
        extc_ref[:, CONV_HEAD - CONV_HIST:CONV_HEAD, :] = extc_ref[:, L + CONV_HEAD - CONV_HIST:L + CONV_HEAD, :]

    M = S * L
    x = x_ref[...].reshape(M, D)
    hn = _rmsnorm(x, g1_ref[...]).astype(bf16)

    u = _dot(hn, win_ref[:, 0:P])
    extp_ref[:, POOL_HEAD:POOL_HEAD + L, :] = u.reshape(S, L, P)

    cg = _dot(hn, win_ref[:, P + C:P + 2 * C])
    v = _dot(hn, win_ref[:, P + 2 * C:P + 3 * C])
    z = (cg * v).reshape(S, L, C)
    extc_ref[:, CONV_HEAD:CONV_HEAD + L, :] = z
    conv = extc_ref[:, CONV_HEAD - 2:CONV_HEAD - 2 + L, :] * convw_ref[0]
    conv = conv + extc_ref[:, CONV_HEAD - 1:CONV_HEAD - 1 + L, :] * convw_ref[1]
    conv = conv + z * convw_ref[2]
    bg = _dot(hn, win_ref[:, P:P + C])
    mix_ref[:, P:P + C] = (bg * conv.reshape(M, C)).astype(bf16)

    for g, w in enumerate(POOL_WINDOWS):
        cols = slice(g * gc, (g + 1) * gc)
        ug = extp_ref[:, POOL_HEAD:POOL_HEAD + L, cols]
        s = ug
        for k in range(1, w):
            s = s + extp_ref[:, POOL_HEAD - k:POOL_HEAD - k + L, cols]
        d = (s * (1.0 / w) - ug).reshape(M, gc).astype(bf16)
        y = _dot(d, wpool_ref[g]) * pscale_ref[:, cols]
        mix_ref[:, cols] = y.astype(bf16)

    h = x + _dot(mix_ref[...], wout_ref[...])
    h_ref[...] = h.reshape(S, L, D)

    @pl.when(j == pl.num_programs(1) - 1)
    def _():
        sp_ref[...] = extp_ref[:, L + POOL_HEAD - POOL_HIST:L + POOL_HEAD, :]
        sc_ref[...] = extc_ref[:, L + CONV_HEAD - CONV_HIST:L + CONV_HEAD, :]


def _mixer_call(x, hist_pool, hist_conv, g1, w_in, w_pool, pool_scale, conv_w, w_out, *,
                seqs_per_block, rows_per_tile):
    B, T, D = x.shape
    P = hist_pool.shape[-1]
    C = hist_conv.shape[-1]
    S, L = seqs_per_block, rows_per_tile
    assert B % S == 0 and T % L == 0 and L % SUBLANES == 0 and L >= POOL_HIST
    shared_hist = hist_pool.shape[0] == 1
    assert shared_hist or hist_pool.shape[0] == B
    assert not shared_hist or S == 1

    def hist_map(b, j):
        return (0, 0, 0) if shared_hist else (b, 0, 0)

    def resident(a):
        return pl.BlockSpec(a.shape, lambda b, j: (0,) * a.ndim, pipeline_mode=pl.Buffered(1))

    weights = (g1, w_in, w_pool, pool_scale, conv_w, w_out)
    weight_bytes = sum(a.size * a.dtype.itemsize for a in weights)
    tile_bytes = S * L * D * 4
    scratch_bytes = S * ((L + POOL_HEAD) * P + (L + CONV_HEAD) * C) * 4 + S * L * (P + C) * 2
    vmem = weight_bytes + 4 * tile_bytes + scratch_bytes + 2 * tile_bytes + 2 * MIB

    return pl.pallas_call(
        _mixer_kernel,
        grid=(B // S, T // L),
        in_specs=[
            pl.BlockSpec((S, L, D), lambda b, j: (b, j, 0)),
            pl.BlockSpec((S, POOL_HIST, P), hist_map),
            pl.BlockSpec((S, CONV_HIST, C), hist_map),
        ] + [resident(a) for a in weights],
        out_specs=[
            pl.BlockSpec((S, L, D), lambda b, j: (b, j, 0)),
            pl.BlockSpec((S, POOL_HIST, P), lambda b, j: (b, 0, 0)),
            pl.BlockSpec((S, CONV_HIST, C), lambda b, j: (b, 0, 0)),
        ],
        out_shape=[
            jax.ShapeDtypeStruct((B, T, D), x.dtype),
            jax.ShapeDtypeStruct((B, POOL_HIST, P), x.dtype),
            jax.ShapeDtypeStruct((B, CONV_HIST, C), x.dtype),
        ],
        scratch_shapes=[
            pltpu.VMEM((S, POOL_HEAD + L, P), jnp.float32),
            pltpu.VMEM((S, CONV_HEAD + L, C), jnp.float32),
            pltpu.VMEM((S * L, P + C), MATMUL_DTYPE),
        ],
        compiler_params=pltpu.CompilerParams(
            dimension_semantics=("arbitrary", "arbitrary"), vmem_limit_bytes=int(vmem)),
        name="mixer",
    )(x, hist_pool, hist_conv, *weights)


def _mlp_kernel(h_ref, g2_ref, wup_ref, wdown_ref, gf_ref, o_ref, *rest, emit_weights, n_chunks):
    hn_ref, a_ref = rest[-2:]
    tm = h_ref.shape[0]
    fc = wup_ref.shape[1]
    slab = min(MLP_SLAB, fc)
    j = pl.program_id(1)

    def step(first, last, parts):
        wu_ref, wd_ref = wup_ref, wdown_ref
        if emit_weights:
            wu_ref, wd_ref = rest[:2]
            wu_ref[...] = wup_ref[...].astype(wu_ref.dtype)
            wd_ref[...] = wdown_ref[...].astype(wd_ref.dtype)
        for p in range(parts):
            rows = slice(p * (tm // parts), (p + 1) * (tm // parts))
            if first:
                hn_ref[rows, :] = _rmsnorm(h_ref[rows, :], g2_ref[...]).astype(hn_ref.dtype)
            for k in range(0, fc, slab):
                a = jnp.maximum(_dot(hn_ref[rows, :], wu_ref[:, k:k + slab]), 0.0)
                a_ref[rows, k:k + slab] = (a * a).astype(a_ref.dtype)
            base_ref = h_ref if first else o_ref
            acc = base_ref[rows, :] + _dot(a_ref[rows, :], wd_ref[...])
            o_ref[rows, :] = _rmsnorm(acc, gf_ref[...]) if last else acc

    if n_chunks == 1:
        step(True, True, MLP_EDGE_PARTS)
    else:
        pl.when(j == 0)(lambda: step(True, False, MLP_EDGE_PARTS))
        pl.when(j == n_chunks - 1)(lambda: step(False, True, MLP_EDGE_PARTS))
        if n_chunks > 2:
            pl.when(jnp.logical_and(j > 0, j < n_chunks - 1))(lambda: step(False, False, 1))


def _mlp_call(h, g2, w_up, w_down, gf, *, rows_per_tile, ff_chunk):
    N, D = h.shape
    F = w_up.shape[1]
    TM, FC = rows_per_tile, ff_chunk
    slab = min(MLP_SLAB, FC)
    assert N % TM == 0 and F % FC == 0 and FC % slab == 0
    emit_weights = w_up.dtype != MATMUL_DTYPE
    assert w_down.dtype == w_up.dtype
    assert not emit_weights or N == TM
    single_tile = N == TM
    mm_bytes = jnp.dtype(MATMUL_DTYPE).itemsize
    tile_bytes = TM * D * 4
    chunk_elems = D * FC
    vmem = ((3 if single_tile else 4) * tile_bytes
            + 4 * chunk_elems * w_up.dtype.itemsize
            + (4 * chunk_elems * mm_bytes if emit_weights else 0)
            + TM * D * mm_bytes + TM * FC * mm_bytes + 3 * TM * slab * 4 + tile_bytes // 2 + 2 * MIB)

    h_spec = pl.BlockSpec((TM, D), lambda i, j: (i, 0),
                          pipeline_mode=pl.Buffered(1) if single_tile else None)
    y_spec = pl.BlockSpec((TM, D), lambda i, j: (i, 0))
    y_shape = jax.ShapeDtypeStruct((N, D), h.dtype)
    if emit_weights:
        out_specs = [y_spec, pl.BlockSpec((D, FC), lambda i, j: (0, j)),
                     pl.BlockSpec((FC, D), lambda i, j: (j, 0))]
        out_shape = [y_shape, jax.ShapeDtypeStruct(w_up.shape, MATMUL_DTYPE),
                     jax.ShapeDtypeStruct(w_down.shape, MATMUL_DTYPE)]
    else:
        out_specs, out_shape = y_spec, y_shape

    return pl.pallas_call(
        functools.partial(_mlp_kernel, emit_weights=emit_weights, n_chunks=F // FC),
        grid=(N // TM, F // FC),
        in_specs=[
            h_spec,
            pl.BlockSpec((1, D), lambda i, j: (0, 0)),
            pl.BlockSpec((D, FC), lambda i, j: (0, j)),
            pl.BlockSpec((FC, D), lambda i, j: (j, 0)),
            pl.BlockSpec((1, D), lambda i, j: (0, 0)),
        ],
        out_specs=out_specs,
        out_shape=out_shape,
        scratch_shapes=[pltpu.VMEM((TM, D), MATMUL_DTYPE), pltpu.VMEM((TM, FC), MATMUL_DTYPE)],
        compiler_params=pltpu.CompilerParams(
            dimension_semantics=("arbitrary", "arbitrary"), vmem_limit_bytes=int(vmem)),
        name="mlp",
    )(h, g2, w_up, w_down, gf)


def kernel(x_prompt, x_sample, cache_pool, cache_conv, meta_tokens, norm1_g, w_in, w_pool,
           pool_scale, conv_w, w_out, norm2_g, w_up, w_down, final_g):
    depth = norm1_g.shape[0]
    assert depth == 1, "history hand-off from the meta tokens is written for a single layer"
    bp, seq, d_model = x_prompt.shape
    bs, dec_seq, _ = x_sample.shape
    n_meta = meta_tokens.shape[0]
    pool_width = cache_pool.shape[-1]
    conv_width = cache_conv.shape[-1]
    assert n_meta >= POOL_HIST, "prompt rows must see only full pooling windows"
    bf16 = MATMUL_DTYPE
    f32 = jnp.float32

    mixer_weights = (norm1_g[0][None], w_in[0].astype(bf16), w_pool[0].astype(bf16),
                     pool_scale[0][None], conv_w[0], w_out[0].astype(bf16))

    _, meta_pool, meta_conv = _mixer_call(
        meta_tokens.astype(x_prompt.dtype)[None],
        jnp.zeros((1, POOL_HIST, pool_width), f32), jnp.zeros((1, CONV_HIST, conv_width), f32),
        *mixer_weights, seqs_per_block=1, rows_per_tile=n_meta)

    hp, sp_pool, sp_conv = _mixer_call(
        x_prompt, meta_pool, meta_conv, *mixer_weights,
        seqs_per_block=1, rows_per_tile=MIXER_ROWS)
    hs, ss_pool, ss_conv = _mixer_call(
        x_sample, cache_pool[0], cache_conv[0], *mixer_weights,
        seqs_per_block=MIXER_ROWS // dec_seq, rows_per_tile=dec_seq)

    g2, gf = norm2_g[0][None], final_g[None]
    y_sample, w_up_mm, w_down_mm = _mlp_call(
        hs.reshape(bs * dec_seq, d_model), g2, w_up[0], w_down[0], gf,
        rows_per_tile=bs * dec_seq, ff_chunk=MLP_FF_CHUNK_ROUNDING)
    y_prompt = _mlp_call(hp.reshape(bp * seq, d_model), g2, w_up_mm, w_down_mm, gf,
                         rows_per_tile=MLP_ROWS, ff_chunk=MLP_FF_CHUNK)
    y_prompt = y_prompt.reshape(bp, seq, d_model)
    y_sample = y_sample.reshape(bs, dec_seq, d_model)

    return (y_prompt, y_sample, sp_pool[None], sp_conv[None], ss_pool[None], ss_conv[None])
```

```python
import functools

import jax
import jax.numpy as jnp
from jax import lax
from jax.experimental import pallas as pl
from jax.experimental.pallas import tpu as pltpu

POOL_WINDOWS = (2, 4, 8, 16)
POOL_HIST = max(POOL_WINDOWS) - 1
CONV_K = 3
CONV_HIST = CONV_K - 1
EPS = 1e-6

SUBLANES = 8
POOL_HEAD = 16
CONV_HEAD = 8
assert POOL_HEAD % SUBLANES == 0 and POOL_HEAD >= POOL_HIST
assert CONV_HEAD % SUBLANES == 0 and CONV_HEAD >= CONV_HIST

MIB = 1024 * 1024

MATMUL_DTYPE = jnp.bfloat16

MIXER_ROWS = 512
MLP_ROWS = 512
MLP_FF_CHUNK = 2048
MLP_SLAB = 512
MLP_FF_CHUNK_ROUNDING = 512
MLP_EDGE_PARTS = 2


def _rmsnorm(x, g):
    r = lax.rsqrt(jnp.mean(x * x, axis=-1, keepdims=True) + EPS)
    return x * r * g


def _dot(a, b):
    return jnp.dot(a, b, preferred_element_type=jnp.float32)


def _mixer_kernel(x_ref, hp_ref, hc_ref, g1_ref, win_ref, wpool_ref, pscale_ref, convw_ref,
                  wout_ref, h_ref, sp_ref, sc_ref, extp_ref, extc_ref, mix_ref):
    S, L, D = x_ref.shape
    P = extp_ref.shape[-1]
    C = extc_ref.shape[-1]
    gc = P // len(POOL_WINDOWS)
    j = pl.program_id(1)
    bf16 = MATMUL_DTYPE

    @pl.when(j == 0)
    def _():
        extp_ref[:, POOL_HEAD - POOL_HIST:POOL_HEAD, :] = hp_ref[...]
        extc_ref[:, CONV_HEAD - CONV_HIST:CONV_HEAD, :] = hc_ref[...]

    @pl.when(j > 0)
    def _():
        extp_ref[:, POOL_HEAD - POOL_HIST:POOL_HEAD, :] = extp_ref[:, L + POOL_HEAD - POOL_HIST:L + POOL_HEAD, :]
        extc_ref[:, CONV_HEAD - CONV_HIST:CONV_HEAD, :] = extc_ref[:, L + CONV_HEAD - CONV_HIST:L + CONV_HEAD, :]

    M = S * L
    x = x_ref[...].reshape(M, D)
    hn = _rmsnorm(x, g1_ref[...]).astype(bf16)

    u = _dot(hn, win_ref[:, 0:P])
    extp_ref[:, POOL_HEAD:POOL_HEAD + L, :] = u.reshape(S, L, P)

    cg = _dot(hn, win_ref[:, P + C:P + 2 * C])
    v = _dot(hn, win_ref[:, P + 2 * C:P + 3 * C])
    z = (cg * v).reshape(S, L, C)
    extc_ref[:, CONV_HEAD:CONV_HEAD + L, :] = z
    conv = extc_ref[:, CONV_HEAD - 2:CONV_HEAD - 2 + L, :] * convw_ref[0:1, :]
    conv = conv + extc_ref[:, CONV_HEAD - 1:CONV_HEAD - 1 + L, :] * convw_ref[1:2, :]
    conv = conv + z * convw_ref[2:3, :]
    bg = _dot(hn, win_ref[:, P:P + C])
    mix_ref[:, P:P + C] = (bg * conv.reshape(M, C)).astype(bf16)

    for g, w in enumerate(POOL_WINDOWS):
        cols = slice(g * gc, (g + 1) * gc)
        ug = extp_ref[:, POOL_HEAD:POOL_HEAD + L, cols]
        s = ug
        for k in range(1, w):
            s = s + extp_ref[:, POOL_HEAD - k:POOL_HEAD - k + L, cols]
        d = (s * (1.0 / w) - ug).reshape(M, gc).astype(bf16)
        y = _dot(d, wpool_ref[g]) * pscale_ref[:, cols]
        mix_ref[:, cols] = y.astype(bf16)

    h = x + _dot(mix_ref[...], wout_ref[...])
    h_ref[...] = h

    @pl.when(j == pl.num_programs(1) - 1)
    def _():
        sp_ref[...] = extp_ref[:, L + POOL_HEAD - POOL_HIST:L + POOL_HEAD, :]
        sc_ref[...] = extc_ref[:, L + CONV_HEAD - CONV_HIST:L + CONV_HEAD, :]


def _mixer_call(x, hist_pool, hist_conv, g1, w_in, w_pool, pool_scale, conv_w, w_out, *,
                seqs_per_block, rows_per_tile):
    B, T, D = x.shape
    P = hist_pool.shape[-1]
    C = hist_conv.shape[-1]
    S, L = seqs_per_block, rows_per_tile
    assert B % S == 0 and T % L == 0 and L % SUBLANES == 0 and L >= POOL_HIST
    shared_hist = hist_pool.shape[1] == 1
    assert shared_hist or hist_pool.shape[1] == B
    assert not shared_hist or S == 1

    def hist_map(b, j):
        return (0, 0, 0, 0) if shared_hist else (0, b, 0, 0)

    def resident(a):
        return pl.BlockSpec(a.shape, lambda b, j: (0,) * a.ndim, pipeline_mode=pl.Buffered(1))

    weights = (g1, w_in, w_pool, pool_scale, conv_w, w_out)
    weight_bytes = sum(a.size * a.dtype.itemsize for a in weights)
    tile_bytes = S * L * D * 4
    scratch_bytes = S * ((L + POOL_HEAD) * P + (L + CONV_HEAD) * C) * 4 + S * L * (P + C) * 2
    vmem = weight_bytes + 4 * tile_bytes + scratch_bytes + 2 * tile_bytes + 2 * MIB

    return pl.pallas_call(
        _mixer_kernel,
        grid=(B // S, T // L),
        in_specs=[
            pl.BlockSpec((S, L, D), lambda b, j: (b, j, 0)),
            pl.BlockSpec((None, S, POOL_HIST, P), hist_map),
            pl.BlockSpec((None, S, CONV_HIST, C), hist_map),
        ] + [resident(a) for a in weights],
        out_specs=[
            pl.BlockSpec((S * L, D), lambda b, j: (b * (T // L) + j, 0)),
            pl.BlockSpec((None, S, POOL_HIST, P), lambda b, j: (0, b, 0, 0)),
            pl.BlockSpec((None, S, CONV_HIST, C), lambda b, j: (0, b, 0, 0)),
        ],
        out_shape=[
            jax.ShapeDtypeStruct((B * T, D), x.dtype),
            jax.ShapeDtypeStruct((1, B, POOL_HIST, P), x.dtype),
            jax.ShapeDtypeStruct((1, B, CONV_HIST, C), x.dtype),
        ],
        scratch_shapes=[
            pltpu.VMEM((S, POOL_HEAD + L, P), jnp.float32),
            pltpu.VMEM((S, CONV_HEAD + L, C), jnp.float32),
            pltpu.VMEM((S * L, P + C), MATMUL_DTYPE),
        ],
        compiler_params=pltpu.CompilerParams(
            dimension_semantics=("arbitrary", "arbitrary"), vmem_limit_bytes=int(vmem)),
        name="mixer",
    )(x, hist_pool, hist_conv, *weights)


def _mlp_chunk(h_ref, g2_ref, gf_ref, o_ref, hn_ref, a_ref, wu_ref, wd_ref, *, first, last, parts):
    tm = h_ref.shape[0]
    fc = wu_ref.shape[1]
    slab = min(MLP_SLAB, fc)
    for p in range(parts):
        rows = slice(p * (tm // parts), (p + 1) * (tm // parts))
        if first:
            hn_ref[rows, :] = _rmsnorm(h_ref[rows, :], g2_ref[...]).astype(hn_ref.dtype)
        for k in range(0, fc, slab):
            a = jnp.maximum(_dot(hn_ref[rows, :], wu_ref[:, k:k + slab]), 0.0)
            a_ref[rows, k:k + slab] = (a * a).astype(a_ref.dtype)
        base_ref = h_ref if first else o_ref
        acc = base_ref[rows, :] + _dot(a_ref[rows, :], wd_ref[...])
        o_ref[rows, :] = _rmsnorm(acc, gf_ref[...]) if last else acc


def _mlp_tile_kernel(h_ref, g2_ref, wup_hbm, wdown_hbm, gf_ref, o_ref, hn_ref, a_ref,
                     wu_buf, wd_buf, sem, *, n_chunks):
    fc = wu_buf.shape[2]
    i = pl.program_id(0)

    def copies(c, slot):
        return (pltpu.make_async_copy(wup_hbm.at[:, pl.ds(c * fc, fc)], wu_buf.at[slot], sem.at[0, slot]),
                pltpu.make_async_copy(wdown_hbm.at[pl.ds(c * fc, fc), :], wd_buf.at[slot], sem.at[1, slot]))

    def start(c, slot):
        for cp in copies(c, slot):
            cp.start()

    def wait(c, slot):
        for cp in copies(c, slot):
            cp.wait()

    pl.when(i == 0)(lambda: start(0, 0))
    for c in range(n_chunks):
        slot = c % 2
        start((c + 1) % n_chunks, 1 - slot)
        wait(c, slot)
        edge = c == 0 or c == n_chunks - 1
        _mlp_chunk(h_ref, g2_ref, gf_ref, o_ref, hn_ref, a_ref, wu_buf.at[slot], wd_buf.at[slot],
                   first=c == 0, last=c == n_chunks - 1, parts=MLP_EDGE_PARTS if edge else 1)
    pl.when(i == pl.num_programs(0) - 1)(lambda: wait(0, 0))


def _mlp_kernel(h_ref, g2_ref, wup_ref, wdown_ref, gf_ref, o_ref, *rest, emit_weights, n_chunks):
    hn_ref, a_ref = rest[-2:]
    j = pl.program_id(1)

    def step(first, last, parts):
        wu_ref, wd_ref = wup_ref, wdown_ref
        if emit_weights:
            wu_ref, wd_ref = rest[:2]
            wu_ref[...] = wup_ref[...].astype(wu_ref.dtype)
            wd_ref[...] = wdown_ref[...].astype(wd_ref.dtype)
        _mlp_chunk(h_ref, g2_ref, gf_ref, o_ref, hn_ref, a_ref, wu_ref, wd_ref,
                   first=first, last=last, parts=parts)

    if n_chunks == 1:
        step(True, True, MLP_EDGE_PARTS)
    else:
        pl.when(j == 0)(lambda: step(True, False, MLP_EDGE_PARTS))
        pl.when(j == n_chunks - 1)(lambda: step(False, True, MLP_EDGE_PARTS))
        if n_chunks > 2:
            pl.when(jnp.logical_and(j > 0, j < n_chunks - 1))(lambda: step(False, False, 1))


def _mlp_call(h, g2, w_up, w_down, gf, *, rows_per_tile, ff_chunk):
    N, D = h.shape
    F = w_up.shape[1]
    TM, FC = rows_per_tile, ff_chunk
    slab = min(MLP_SLAB, FC)
    assert N % TM == 0 and F % FC == 0 and FC % slab == 0
    emit_weights = w_up.dtype != MATMUL_DTYPE
    assert w_down.dtype == w_up.dtype
    assert not emit_weights or N == TM
    single_tile = N == TM
    mm_bytes = jnp.dtype(MATMUL_DTYPE).itemsize
    tile_bytes = TM * D * 4
    chunk_elems = D * FC
    vmem = ((3 if single_tile else 4) * tile_bytes
            + 4 * chunk_elems * w_up.dtype.itemsize
            + (4 * chunk_elems * mm_bytes if emit_weights else 0)
            + TM * D * mm_bytes + TM * FC * mm_bytes + 3 * TM * slab * 4 + tile_bytes // 2 + 2 * MIB)

    hn_a_scratch = [pltpu.VMEM((TM, D), MATMUL_DTYPE), pltpu.VMEM((TM, FC), MATMUL_DTYPE)]
    n_chunks = F // FC
    if not emit_weights and n_chunks % 2 == 0:
        vec_spec = pl.BlockSpec((1, D), lambda i: (0, 0))
        tile_spec = pl.BlockSpec((TM, D), lambda i: (i, 0))
        return pl.pallas_call(
            functools.partial(_mlp_tile_kernel, n_chunks=n_chunks),
            grid=(N // TM,),
            in_specs=[tile_spec, vec_spec, pl.BlockSpec(memory_space=pl.ANY),
                      pl.BlockSpec(memory_space=pl.ANY), vec_spec],
            out_specs=tile_spec,
            out_shape=jax.ShapeDtypeStruct((N, D), h.dtype),
            scratch_shapes=hn_a_scratch + [
                pltpu.VMEM((2, D, FC), MATMUL_DTYPE), pltpu.VMEM((2, FC, D), MATMUL_DTYPE),
                pltpu.SemaphoreType.DMA((2, 2))],
            compiler_params=pltpu.CompilerParams(
                dimension_semantics=("arbitrary",), vmem_limit_bytes=int(vmem)),
            name="mlp_tiles",
        )(h, g2, w_up, w_down, gf)

    h_spec = pl.BlockSpec((TM, D), lambda i, j: (i, 0),
                          pipeline_mode=pl.Buffered(1) if single_tile else None)
    y_spec = pl.BlockSpec((TM, D), lambda i, j: (i, 0))
    y_shape = jax.ShapeDtypeStruct((N, D), h.dtype)
    if emit_weights:
        out_specs = [y_spec, pl.BlockSpec((D, FC), lambda i, j: (0, j)),
                     pl.BlockSpec((FC, D), lambda i, j: (j, 0))]
        out_shape = [y_shape, jax.ShapeDtypeStruct(w_up.shape, MATMUL_DTYPE),
                     jax.ShapeDtypeStruct(w_down.shape, MATMUL_DTYPE)]
    else:
        out_specs, out_shape = y_spec, y_shape

    return pl.pallas_call(
        functools.partial(_mlp_kernel, emit_weights=emit_weights, n_chunks=n_chunks),
        grid=(N // TM, n_chunks),
        in_specs=[
            h_spec,
            pl.BlockSpec((1, D), lambda i, j: (0, 0)),
            pl.BlockSpec((D, FC), lambda i, j: (0, j)),
            pl.BlockSpec((FC, D), lambda i, j: (j, 0)),
            pl.BlockSpec((1, D), lambda i, j: (0, 0)),
        ],
        out_specs=out_specs,
        out_shape=out_shape,
        scratch_shapes=hn_a_scratch,
        compiler_params=pltpu.CompilerParams(
            dimension_semantics=("arbitrary", "arbitrary"), vmem_limit_bytes=int(vmem)),
        name="mlp",
    )(h, g2, w_up, w_down, gf)


def kernel(x_prompt, x_sample, cache_pool, cache_conv, meta_tokens, norm1_g, w_in, w_pool,
           pool_scale, conv_w, w_out, norm2_g, w_up, w_down, final_g):
    depth = norm1_g.shape[0]
    assert depth == 1, "history hand-off from the meta tokens is written for a single layer"
    bp, seq, d_model = x_prompt.shape
    bs, dec_seq, _ = x_sample.shape
    n_meta = meta_tokens.shape[0]
    pool_width = cache_pool.shape[-1]
    conv_width = cache_conv.shape[-1]
    assert n_meta >= POOL_HIST, "prompt rows must see only full pooling windows"
    bf16 = MATMUL_DTYPE
    f32 = jnp.float32

    mixer_weights = (norm1_g[0][None], w_in[0].astype(bf16), w_pool[0].astype(bf16),
                     pool_scale[0][None], conv_w[0], w_out[0].astype(bf16))

    _, meta_pool, meta_conv = _mixer_call(
        meta_tokens.astype(x_prompt.dtype)[None],
        jnp.zeros((1, 1, POOL_HIST, pool_width), f32), jnp.zeros((1, 1, CONV_HIST, conv_width), f32),
        *mixer_weights, seqs_per_block=1, rows_per_tile=n_meta)

    hp, sp_pool, sp_conv = _mixer_call(
        x_prompt, meta_pool, meta_conv, *mixer_weights,
        seqs_per_block=1, rows_per_tile=MIXER_ROWS)
    hs, ss_pool, ss_conv = _mixer_call(
        x_sample, cache_pool, cache_conv, *mixer_weights,
        seqs_per_block=MIXER_ROWS // dec_seq, rows_per_tile=dec_seq)

    g2, gf = norm2_g[0][None], final_g[None]
    y_sample, w_up_mm, w_down_mm = _mlp_call(
        hs, g2, w_up[0], w_down[0], gf,
        rows_per_tile=bs * dec_seq, ff_chunk=MLP_FF_CHUNK_ROUNDING)
    y_prompt = _mlp_call(hp, g2, w_up_mm, w_down_mm, gf,
                         rows_per_tile=MLP_ROWS, ff_chunk=MLP_FF_CHUNK)
    y_prompt = y_prompt.reshape(bp, seq, d_model)
    y_sample = y_sample.reshape(bs, dec_seq, d_model)

    return (y_prompt, y_sample, sp_pool, sp_conv, ss_pool, ss_conv)
```

```python
import functools

import jax
import jax.numpy as jnp
from jax import lax
from jax.experimental import pallas as pl
from jax.experimental.pallas import tpu as pltpu

POOL_WINDOWS = (2, 4, 8, 16)
POOL_HIST = max(POOL_WINDOWS) - 1
CONV_K = 3
CONV_HIST = CONV_K - 1
EPS = 1e-6

SUBLANES = 8
POOL_HEAD = 16
CONV_HEAD = 8
assert POOL_HEAD % SUBLANES == 0 and POOL_HEAD >= POOL_HIST
assert CONV_HEAD % SUBLANES == 0 and CONV_HEAD >= CONV_HIST

MIB = 1024 * 1024

MATMUL_DTYPE = jnp.bfloat16

MIXER_ROWS = 512
MLP_ROWS = 512
MLP_FF_CHUNK = 2048
MLP_SLAB = 512
MLP_FF_CHUNK_ROUNDING = 512


def _rmsnorm(x, g):
    r = lax.rsqrt(jnp.mean(x * x, axis=-1, keepdims=True) + EPS)
    return x * r * g


def _dot(a, b):
    return jnp.dot(a, b, preferred_element_type=jnp.float32)


def _mixer_kernel(x_ref, hp_ref, hc_ref, g1_ref, win_ref, wpool_ref, pscale_ref, convw_ref,
                  wout_ref, h_ref, sp_ref, sc_ref, extp_ref, extc_ref, mix_ref):
    S, L, D = x_ref.shape
    P = extp_ref.shape[-1]
    C = extc_ref.shape[-1]
    gc = P // len(POOL_WINDOWS)
    j = pl.program_id(1)
    bf16 = MATMUL_DTYPE

    @pl.when(j == 0)
    def _():
        for r in range(POOL_HIST):
            extp_ref[:, POOL_HEAD - POOL_HIST + r, :] = hp_ref[r]
        extc_ref[:, CONV_HEAD - CONV_HIST:CONV_HEAD, :] = hc_ref[...]

    @pl.when(j > 0)
    def _():
        extp_ref[:, POOL_HEAD - POOL_HIST:POOL_HEAD, :] = extp_ref[:, L + POOL_HEAD - POOL_HIST:L + POOL_HEAD, :]
        extc_ref[:, CONV_HEAD - CONV_HIST:CONV_HEAD, :] = extc_ref[:, L + CONV_HEAD - CONV_HIST:L + CONV_HEAD, :]

    M = S * L
    x = x_ref[...].reshape(M, D)
    hn = _rmsnorm(x, g1_ref[...]).astype(bf16)

    u = _dot(hn, win_ref[:, 0:P])
    extp_ref[:, POOL_HEAD:POOL_HEAD + L, :] = u.reshape(S, L, P)

    cg = _dot(hn, win_ref[:, P + C:P + 2 * C])
    v = _dot(hn, win_ref[:, P + 2 * C:P + 3 * C])
    z = (cg * v).reshape(S, L, C)
    extc_ref[:, CONV_HEAD:CONV_HEAD + L, :] = z
    conv = extc_ref[:, CONV_HEAD - 2:CONV_HEAD - 2 + L, :] * convw_ref[0:1, :]
    conv = conv + extc_ref[:, CONV_HEAD - 1:CONV_HEAD - 1 + L, :] * convw_ref[1:2, :]
    conv = conv + z * convw_ref[2:3, :]
    bg = _dot(hn, win_ref[:, P:P + C])
    mix_ref[:, P:P + C] = (bg * conv.reshape(M, C)).astype(bf16)

    for g, w in enumerate(POOL_WINDOWS):
        cols = slice(g * gc, (g + 1) * gc)
        ug = extp_ref[:, POOL_HEAD:POOL_HEAD + L, cols]
        s = ug
        for k in range(1, w):
            s = s + extp_ref[:, POOL_HEAD - k:POOL_HEAD - k + L, cols]
        d = (s * (1.0 / w) - ug).reshape(M, gc).astype(bf16)
        y = _dot(d, wpool_ref[g]) * pscale_ref[:, cols]
        mix_ref[:, cols] = y.astype(bf16)

    h = x + _dot(mix_ref[...], wout_ref[...])
    h_ref[...] = h

    @pl.when(j == pl.num_programs(1) - 1)
    def _():
        for r in range(POOL_HIST):
            sp_ref[r, pl.ds(pl.program_id(0) * S, S), :] = extp_ref[:, L + POOL_HEAD - POOL_HIST + r, :]
        sc_ref[...] = extc_ref[:, L + CONV_HEAD - CONV_HIST:L + CONV_HEAD, :]


def _mixer_call(x, hist_pool, hist_conv, g1, w_in, w_pool, pool_scale, conv_w, w_out, *,
                seqs_per_block, rows_per_tile):
    B, T, D = x.shape
    P = hist_pool.shape[-1]
    C = hist_conv.shape[-1]
    S, L = seqs_per_block, rows_per_tile
    assert B % S == 0 and T % L == 0 and L % SUBLANES == 0 and L >= POOL_HIST
    shared_hist = hist_pool.shape[1] == 1
    assert shared_hist or hist_pool.shape[1] == B
    assert hist_conv.shape[1] == hist_pool.shape[1]
    assert not shared_hist or S == 1

    def hist_pool_map(b, j):
        return (0, 0, 0) if shared_hist else (0, b, 0)

    def hist_conv_map(b, j):
        return (0, 0, 0, 0) if shared_hist else (0, b, 0, 0)

    def resident(a):
        return pl.BlockSpec(a.shape, lambda b, j: (0,) * a.ndim, pipeline_mode=pl.Buffered(1))

    weights = (g1, w_in, w_pool, pool_scale, conv_w, w_out)
    weight_bytes = sum(a.size * a.dtype.itemsize for a in weights)
    tile_bytes = S * L * D * 4
    scratch_bytes = S * ((L + POOL_HEAD) * P + (L + CONV_HEAD) * C) * 4 + S * L * (P + C) * 2
    state_bytes = 2 * POOL_HIST * B * P * 4
    vmem = weight_bytes + 4 * tile_bytes + scratch_bytes + state_bytes + 2 * tile_bytes + 2 * MIB

    return pl.pallas_call(
        _mixer_kernel,
        grid=(B // S, T // L),
        in_specs=[
            pl.BlockSpec((S, L, D), lambda b, j: (b, j, 0)),
            pl.BlockSpec((POOL_HIST, S, P), hist_pool_map),
            pl.BlockSpec((None, S, CONV_HIST, C), hist_conv_map),
        ] + [resident(a) for a in weights],
        out_specs=[
            pl.BlockSpec((S * L, D), lambda b, j: (b * (T // L) + j, 0)),
            pl.BlockSpec((POOL_HIST, B, P), lambda b, j: (0, 0, 0)),
            pl.BlockSpec((None, S, CONV_HIST, C), lambda b, j: (0, b, 0, 0)),
        ],
        out_shape=[
            jax.ShapeDtypeStruct((B * T, D), x.dtype),
            jax.ShapeDtypeStruct((POOL_HIST, B, P), x.dtype),
            jax.ShapeDtypeStruct((1, B, CONV_HIST, C), x.dtype),
        ],
        scratch_shapes=[
            pltpu.VMEM((S, POOL_HEAD + L, P), jnp.float32),
            pltpu.VMEM((S, CONV_HEAD + L, C), jnp.float32),
            pltpu.VMEM((S * L, P + C), MATMUL_DTYPE),
        ],
        compiler_params=pltpu.CompilerParams(
            dimension_semantics=("arbitrary", "arbitrary"), vmem_limit_bytes=int(vmem)),
        name="mixer",
    )(x, hist_pool, hist_conv, *weights)


def _mlp_chunk(h_ref, g2_ref, gf_ref, o_ref, hn_ref, a_ref, wu_ref, wd_ref, *, first, last):
    fc = wu_ref.shape[1]
    slab = min(MLP_SLAB, fc)
    if first:
        hn_ref[...] = _rmsnorm(h_ref[...], g2_ref[...]).astype(hn_ref.dtype)
    for k in range(0, fc, slab):
        a = jnp.maximum(_dot(hn_ref[...], wu_ref[:, k:k + slab]), 0.0)
        a_ref[:, k:k + slab] = (a * a).astype(a_ref.dtype)
    base_ref = h_ref if first else o_ref
    acc = base_ref[...] + _dot(a_ref[...], wd_ref[...])
    o_ref[...] = _rmsnorm(acc, gf_ref[...]) if last else acc


def _mlp_tile_kernel(h_ref, g2_ref, wup_hbm, wdown_hbm, gf_ref, o_ref, hn_ref, a_ref,
                     wu_buf, wd_buf, sem, *, n_chunks):
    fc = wu_buf.shape[2]
    i = pl.program_id(0)

    def copies(c, slot):
        return (pltpu.make_async_copy(wup_hbm.at[c], wu_buf.at[slot], sem.at[0, slot]),
                pltpu.make_async_copy(wdown_hbm.at[pl.ds(c * fc, fc), :], wd_buf.at[slot], sem.at[1, slot]))

    def start(c, slot):
        for cp in copies(c, slot):
            cp.start()

    def wait(c, slot):
        for cp in copies(c, slot):
            cp.wait()

    pl.when(i == 0)(lambda: start(0, 0))
    for c in range(n_chunks):
        slot = c % 2
        start((c + 1) % n_chunks, 1 - slot)
        wait(c, slot)
        _mlp_chunk(h_ref, g2_ref, gf_ref, o_ref, hn_ref, a_ref, wu_buf.at[slot], wd_buf.at[slot],
                   first=c == 0, last=c == n_chunks - 1)
    pl.when(i == pl.num_programs(0) - 1)(lambda: wait(0, 0))


def _mlp_kernel(h_ref, g2_ref, wup_ref, wdown_ref, gf_ref, o_ref, wu_ref, wd_ref, hn_ref, a_ref,
                *, n_chunks):
    j = pl.program_id(1)

    def step(first, last):
        wu_ref[...] = wup_ref[...].astype(wu_ref.dtype)
        wd_ref[...] = wdown_ref[...].astype(wd_ref.dtype)
        _mlp_chunk(h_ref, g2_ref, gf_ref, o_ref, hn_ref, a_ref, wu_ref, wd_ref,
                   first=first, last=last)

    if n_chunks == 1:
        step(True, True)
    else:
        pl.when(j == 0)(lambda: step(True, False))
        pl.when(j == n_chunks - 1)(lambda: step(False, True))
        if n_chunks > 2:
            pl.when(jnp.logical_and(j > 0, j < n_chunks - 1))(lambda: step(False, False))


def _mlp_vmem_bytes(TM, D, FC, weight_itemsize, rounding):
    mm_bytes = jnp.dtype(MATMUL_DTYPE).itemsize
    tile_bytes = TM * D * 4
    chunk_elems = D * FC
    return ((3 if rounding else 4) * tile_bytes + 4 * chunk_elems * weight_itemsize
            + (4 * chunk_elems * mm_bytes if rounding else 0)
            + TM * D * mm_bytes + TM * FC * mm_bytes + 3 * TM * min(MLP_SLAB, FC) * 4
            + tile_bytes // 2 + 2 * MIB)


def _mlp_scratch(TM, D, FC):
    return [pltpu.VMEM((TM, D), MATMUL_DTYPE), pltpu.VMEM((TM, FC), MATMUL_DTYPE)]


def _mlp_tiles_call(h, g2, w_up, w_down, gf, *, rows_per_tile):
    N, D = h.shape
    n_chunks, _, FC = w_up.shape
    TM = rows_per_tile
    assert N % TM == 0 and n_chunks % 2 == 0 and w_down.shape == (n_chunks * FC, D)
    assert w_up.dtype == MATMUL_DTYPE and w_down.dtype == MATMUL_DTYPE
    vec_spec = pl.BlockSpec((1, D), lambda i: (0, 0))
    tile_spec = pl.BlockSpec((TM, D), lambda i: (i, 0))
    vmem = _mlp_vmem_bytes(TM, D, FC, w_up.dtype.itemsize, rounding=False)
    return pl.pallas_call(
        functools.partial(_mlp_tile_kernel, n_chunks=n_chunks),
        grid=(N // TM,),
        in_specs=[tile_spec, vec_spec, pl.BlockSpec(memory_space=pl.ANY),
                  pl.BlockSpec(memory_space=pl.ANY), vec_spec],
        out_specs=tile_spec,
        out_shape=jax.ShapeDtypeStruct((N, D), h.dtype),
        scratch_shapes=_mlp_scratch(TM, D, FC) + [
            pltpu.VMEM((2, D, FC), MATMUL_DTYPE), pltpu.VMEM((2, FC, D), MATMUL_DTYPE),
            pltpu.SemaphoreType.DMA((2, 2))],
        compiler_params=pltpu.CompilerParams(
            dimension_semantics=("arbitrary",), vmem_limit_bytes=int(vmem)),
        name="mlp_tiles",
    )(h, g2, w_up, w_down, gf)


def _mlp_rounding_call(h, g2, w_up, w_down, gf, *, ff_chunk, out_chunk):
    N, D = h.shape
    F = w_up.shape[1]
    FC = ff_chunk
    assert F % out_chunk == 0 and out_chunk % FC == 0 and FC % min(MLP_SLAB, FC) == 0
    assert w_down.dtype == w_up.dtype
    per_out = out_chunk // FC
    n_chunks = F // FC
    vmem = _mlp_vmem_bytes(N, D, FC, w_up.dtype.itemsize, rounding=True)
    tile_in = pl.BlockSpec((N, D), lambda i, j: (0, 0), pipeline_mode=pl.Buffered(1))
    vec_spec = pl.BlockSpec((1, D), lambda i, j: (0, 0))
    return pl.pallas_call(
        functools.partial(_mlp_kernel, n_chunks=n_chunks),
        grid=(1, n_chunks),
        in_specs=[
            tile_in, vec_spec,
            pl.BlockSpec((D, FC), lambda i, j: (0, j)),
            pl.BlockSpec((FC, D), lambda i, j: (j, 0)),
            vec_spec,
        ],
        out_specs=[
            pl.BlockSpec((N, D), lambda i, j: (0, 0)),
            pl.BlockSpec((None, D, FC), lambda i, j: (j // per_out, 0, j % per_out)),
            pl.BlockSpec((FC, D), lambda i, j: (j, 0)),
        ],
        out_shape=[
            jax.ShapeDtypeStruct((N, D), h.dtype),
            jax.ShapeDtypeStruct((F // out_chunk, D, out_chunk), MATMUL_DTYPE),
            jax.ShapeDtypeStruct((F, D), MATMUL_DTYPE),
        ],
        scratch_shapes=_mlp_scratch(N, D, FC),
        compiler_params=pltpu.CompilerParams(
            dimension_semantics=("arbitrary", "arbitrary"), vmem_limit_bytes=int(vmem)),
        name="mlp",
    )(h, g2, w_up, w_down, gf)


def kernel(x_prompt, x_sample, cache_pool, cache_conv, meta_tokens, norm1_g, w_in, w_pool,
           pool_scale, conv_w, w_out, norm2_g, w_up, w_down, final_g):
    depth = norm1_g.shape[0]
    assert depth == 1, "history hand-off from the meta tokens is written for a single layer"
    bp, seq, d_model = x_prompt.shape
    bs, dec_seq, _ = x_sample.shape
    n_meta = meta_tokens.shape[0]
    pool_width = cache_pool.shape[-1]
    conv_width = cache_conv.shape[-1]
    assert n_meta >= POOL_HIST, "prompt rows must see only full pooling windows"
    bf16 = MATMUL_DTYPE
    f32 = jnp.float32

    mixer_weights = (norm1_g[0][None], w_in[0].astype(bf16), w_pool[0].astype(bf16),
                     pool_scale[0][None], conv_w[0], w_out[0].astype(bf16))

    _, meta_pool, meta_conv = _mixer_call(
        meta_tokens.astype(x_prompt.dtype)[None],
        jnp.zeros((POOL_HIST, 1, pool_width), f32), jnp.zeros((1, 1, CONV_HIST, conv_width), f32),
        *mixer_weights, seqs_per_block=1, rows_per_tile=n_meta)

    hp, sp_pool, sp_conv = _mixer_call(
        x_prompt, meta_pool, meta_conv, *mixer_weights,
        seqs_per_block=1, rows_per_tile=MIXER_ROWS)
    hs, ss_pool, ss_conv = _mixer_call(
        x_sample, jnp.transpose(cache_pool[0], (1, 0, 2)), cache_conv, *mixer_weights,
        seqs_per_block=MIXER_ROWS // dec_seq, rows_per_tile=dec_seq)

    g2, gf = norm2_g[0][None], final_g[None]
    y_sample, w_up_mm, w_down_mm = _mlp_rounding_call(
        hs, g2, w_up[0], w_down[0], gf, ff_chunk=MLP_FF_CHUNK_ROUNDING, out_chunk=MLP_FF_CHUNK)
    y_prompt = _mlp_tiles_call(hp, g2, w_up_mm, w_down_mm, gf, rows_per_tile=MLP_ROWS)
    y_prompt = y_prompt.reshape(bp, seq, d_model)
    y_sample = y_sample.reshape(bs, dec_seq, d_model)

    sp_pool = jnp.transpose(sp_pool, (1, 0, 2))[None]
    ss_pool = jnp.transpose(ss_pool, (1, 0, 2))[None]
    return (y_prompt, y_sample, sp_pool, sp_conv, ss_pool, ss_conv)
```

```python
import functools

import jax
import jax.numpy as jnp
from jax import lax
from jax.experimental import pallas as pl
from jax.experimental.pallas import tpu as pltpu

POOL_WINDOWS = (2, 4, 8, 16)
POOL_HIST = max(POOL_WINDOWS) - 1
CONV_K = 3
CONV_HIST = CONV_K - 1
EPS = 1e-6

SUBLANES = 8
POOL_HEAD = 16
CONV_HEAD = 8
assert POOL_HEAD % SUBLANES == 0 and POOL_HEAD >= POOL_HIST
assert CONV_HEAD % SUBLANES == 0 and CONV_HEAD >= CONV_HIST

MIB = 1024 * 1024

MATMUL_DTYPE = jnp.bfloat16

MIXER_ROWS = 512
MLP_ROWS = 512
MLP_FF_CHUNK = 2048
MLP_SLAB = 512
MLP_FF_CHUNK_ROUNDING = 512


def _rmsnorm(x, g):
    r = lax.rsqrt(jnp.mean(x * x, axis=-1, keepdims=True) + EPS)
    return x * r * g


def _dot(a, b):
    return jnp.dot(a, b, preferred_element_type=jnp.float32)


def _mixer_kernel(x_ref, hp_ref, hc_ref, g1_ref, win_ref, wpool_ref, pscale_ref, convw_ref,
                  wout_ref, *refs, n_side):
    side_in = refs[:n_side]
    h_ref, sp_ref, sc_ref = refs[n_side:n_side + 3]
    side_out = refs[n_side + 3:2 * n_side + 3]
    extp_ref, extc_ref, mix_ref = refs[2 * n_side + 3:]
    S, L, D = x_ref.shape
    P = extp_ref.shape[-1]
    C = extc_ref.shape[-1]
    gc = P // len(POOL_WINDOWS)
    j = pl.program_id(1)
    bf16 = MATMUL_DTYPE

    @pl.when(j == 0)
    def _():
        for r in range(POOL_HIST):
            extp_ref[:, POOL_HEAD - POOL_HIST + r, :] = hp_ref[r]
        extc_ref[:, CONV_HEAD - CONV_HIST:CONV_HEAD, :] = hc_ref[...]

    @pl.when(j > 0)
    def _():
        extp_ref[:, POOL_HEAD - POOL_HIST:POOL_HEAD, :] = extp_ref[:, L + POOL_HEAD - POOL_HIST:L + POOL_HEAD, :]
        extc_ref[:, CONV_HEAD - CONV_HIST:CONV_HEAD, :] = extc_ref[:, L + CONV_HEAD - CONV_HIST:L + CONV_HEAD, :]

    for src, dst in zip(side_in, side_out):
        dst[...] = src[...].astype(dst.dtype)

    M = S * L
    x = x_ref[...].reshape(M, D)
    hn = _rmsnorm(x, g1_ref[...]).astype(bf16)

    u = _dot(hn, win_ref[:, 0:P])
    extp_ref[:, POOL_HEAD:POOL_HEAD + L, :] = u.reshape(S, L, P)

    cg = _dot(hn, win_ref[:, P + C:P + 2 * C])
    v = _dot(hn, win_ref[:, P + 2 * C:P + 3 * C])
    z = (cg * v).reshape(S, L, C)
    extc_ref[:, CONV_HEAD:CONV_HEAD + L, :] = z
    conv = extc_ref[:, CONV_HEAD - 2:CONV_HEAD - 2 + L, :] * convw_ref[0:1, :]
    conv = conv + extc_ref[:, CONV_HEAD - 1:CONV_HEAD - 1 + L, :] * convw_ref[1:2, :]
    conv = conv + z * convw_ref[2:3, :]
    bg = _dot(hn, win_ref[:, P:P + C])
    mix_ref[:, P:P + C] = (bg * conv.reshape(M, C)).astype(bf16)

    for g, w in enumerate(POOL_WINDOWS):
        cols = slice(g * gc, (g + 1) * gc)
        ug = extp_ref[:, POOL_HEAD:POOL_HEAD + L, cols]
        s = ug
        for k in range(1, w):
            s = s + extp_ref[:, POOL_HEAD - k:POOL_HEAD - k + L, cols]
        d = (s * (1.0 / w) - ug).reshape(M, gc).astype(bf16)
        y = _dot(d, wpool_ref[g]) * pscale_ref[:, cols]
        mix_ref[:, cols] = y.astype(bf16)

    h = x + _dot(mix_ref[...], wout_ref[...])
    h_ref[...] = h

    @pl.when(j == pl.num_programs(1) - 1)
    def _():
        for r in range(POOL_HIST):
            sp_ref[r, pl.ds(pl.program_id(0) * S, S), :] = extp_ref[:, L + POOL_HEAD - POOL_HIST + r, :]
        sc_ref[...] = extc_ref[:, L + CONV_HEAD - CONV_HIST:L + CONV_HEAD, :]


def _mixer_call(x, hist_pool, hist_conv, g1, w_in, w_pool, pool_scale, conv_w, w_out, *,
                seqs_per_block, rows_per_tile, side_round=()):
    B, T, D = x.shape
    P = hist_pool.shape[-1]
    C = hist_conv.shape[-1]
    S, L = seqs_per_block, rows_per_tile
    assert B % S == 0 and T % L == 0 and L % SUBLANES == 0 and L >= POOL_HIST
    shared_hist = hist_pool.shape[1] == 1
    assert shared_hist or hist_pool.shape[1] == B
    assert hist_conv.shape[1] == hist_pool.shape[1]
    assert not shared_hist or S == 1

    def hist_pool_map(b, j):
        return (0, 0, 0) if shared_hist else (0, b, 0)

    def hist_conv_map(b, j):
        return (0, 0, 0, 0) if shared_hist else (0, b, 0, 0)

    def resident(a):
        return pl.BlockSpec(a.shape, lambda b, j: (0,) * a.ndim, pipeline_mode=pl.Buffered(1))

    weights = (g1, w_in, w_pool, pool_scale, conv_w, w_out)
    weight_bytes = sum(a.size * a.dtype.itemsize for a in weights)
    tile_bytes = S * L * D * 4
    scratch_bytes = S * ((L + POOL_HEAD) * P + (L + CONV_HEAD) * C) * 4 + S * L * (P + C) * 2
    state_bytes = 2 * POOL_HIST * B * P * 4
    vmem = weight_bytes + 4 * tile_bytes + scratch_bytes + state_bytes + 2 * tile_bytes + 2 * MIB

    n_steps = (B // S) * (T // L)
    side_arrays, side_specs, side_out_shapes = [], [], []
    for a, axis in side_round:
        assert a.ndim == 2 and a.shape[axis] % n_steps == 0
        block = tuple(a.shape[d] // n_steps if d == axis else a.shape[d] for d in range(2))
        side_arrays.append(a)
        side_specs.append(pl.BlockSpec(block, lambda b, j, _axis=axis: tuple(
            b * (T // L) + j if d == _axis else 0 for d in range(2))))
        side_out_shapes.append(jax.ShapeDtypeStruct(a.shape, MATMUL_DTYPE))
        vmem += 2 * block[0] * block[1] * (a.dtype.itemsize + jnp.dtype(MATMUL_DTYPE).itemsize)

    return pl.pallas_call(
        functools.partial(_mixer_kernel, n_side=len(side_arrays)),
        grid=(B // S, T // L),
        in_specs=[
            pl.BlockSpec((S, L, D), lambda b, j: (b, j, 0)),
            pl.BlockSpec((POOL_HIST, S, P), hist_pool_map),
            pl.BlockSpec((None, S, CONV_HIST, C), hist_conv_map),
        ] + [resident(a) for a in weights] + side_specs,
        out_specs=[
            pl.BlockSpec((S * L, D), lambda b, j: (b * (T // L) + j, 0)),
            pl.BlockSpec((POOL_HIST, B, P), lambda b, j: (0, 0, 0)),
            pl.BlockSpec((None, S, CONV_HIST, C), lambda b, j: (0, b, 0, 0)),
        ] + side_specs,
        out_shape=[
            jax.ShapeDtypeStruct((B * T, D), x.dtype),
            jax.ShapeDtypeStruct((POOL_HIST, B, P), x.dtype),
            jax.ShapeDtypeStruct((1, B, CONV_HIST, C), x.dtype),
        ] + side_out_shapes,
        scratch_shapes=[
            pltpu.VMEM((S, POOL_HEAD + L, P), jnp.float32),
            pltpu.VMEM((S, CONV_HEAD + L, C), jnp.float32),
            pltpu.VMEM((S * L, P + C), MATMUL_DTYPE),
        ],
        compiler_params=pltpu.CompilerParams(
            dimension_semantics=("arbitrary", "arbitrary"), vmem_limit_bytes=int(vmem)),
        name="mixer",
    )(x, hist_pool, hist_conv, *weights, *side_arrays)


def _mlp_chunk(h_ref, g2_ref, gf_ref, o_ref, hn_ref, a_ref, wu_ref, wd_ref, *, first, last):
    fc = wu_ref.shape[1]
    slab = min(MLP_SLAB, fc)
    if first:
        hn_ref[...] = _rmsnorm(h_ref[...], g2_ref[...]).astype(hn_ref.dtype)
    for k in range(0, fc, slab):
        a = jnp.maximum(_dot(hn_ref[...], wu_ref[:, k:k + slab]), 0.0)
        a_ref[:, k:k + slab] = (a * a).astype(a_ref.dtype)
    base_ref = h_ref if first else o_ref
    acc = base_ref[...] + _dot(a_ref[...], wd_ref[...])
    o_ref[...] = _rmsnorm(acc, gf_ref[...]) if last else acc


def _mlp_tile_kernel(h_ref, g2_ref, wup_hbm, wdown_hbm, gf_ref, o_ref, hn_ref, a_ref,
                     wu_buf, wd_buf, sem, *, n_chunks):
    fc = wu_buf.shape[2]
    i = pl.program_id(0)

    def copies(c, slot):
        return (pltpu.make_async_copy(wup_hbm.at[:, pl.ds(c * fc, fc)], wu_buf.at[slot], sem.at[0, slot]),
                pltpu.make_async_copy(wdown_hbm.at[pl.ds(c * fc, fc), :], wd_buf.at[slot], sem.at[1, slot]))

    def start(c, slot):
        for cp in copies(c, slot):
            cp.start()

    def wait(c, slot):
        for cp in copies(c, slot):
            cp.wait()

    pl.when(i == 0)(lambda: start(0, 0))
    for c in range(n_chunks):
        slot = c % 2
        start((c + 1) % n_chunks, 1 - slot)
        wait(c, slot)
        _mlp_chunk(h_ref, g2_ref, gf_ref, o_ref, hn_ref, a_ref, wu_buf.at[slot], wd_buf.at[slot],
                   first=c == 0, last=c == n_chunks - 1)
    pl.when(i == pl.num_programs(0) - 1)(lambda: wait(0, 0))


def _mlp_kernel(h_ref, g2_ref, wup_ref, wdown_ref, gf_ref, o_ref, *rest, emit_weights, n_chunks):
    hn_ref, a_ref = rest[-2:]
    j = pl.program_id(1)

    def step(first, last):
        wu_ref, wd_ref = wup_ref, wdown_ref
        if emit_weights:
            wu_ref, wd_ref = rest[:2]
            wu_ref[...] = wup_ref[...].astype(wu_ref.dtype)
            wd_ref[...] = wdown_ref[...].astype(wd_ref.dtype)
        _mlp_chunk(h_ref, g2_ref, gf_ref, o_ref, hn_ref, a_ref, wu_ref, wd_ref,
                   first=first, last=last)

    if n_chunks == 1:
        step(True, True)
    else:
        pl.when(j == 0)(lambda: step(True, False))
        pl.when(j == n_chunks - 1)(lambda: step(False, True))
        if n_chunks > 2:
            pl.when(jnp.logical_and(j > 0, j < n_chunks - 1))(lambda: step(False, False))


def _mlp_call(h, g2, w_up, w_down, gf, *, rows_per_tile, ff_chunk):
    N, D = h.shape
    F = w_up.shape[1]
    TM, FC = rows_per_tile, ff_chunk
    slab = min(MLP_SLAB, FC)
    assert N % TM == 0 and F % FC == 0 and FC % slab == 0
    emit_weights = w_up.dtype != MATMUL_DTYPE
    assert w_down.dtype == w_up.dtype
    assert not emit_weights or N == TM
    single_tile = N == TM
    mm_bytes = jnp.dtype(MATMUL_DTYPE).itemsize
    tile_bytes = TM * D * 4
    chunk_elems = D * FC
    vmem = ((3 if single_tile else 4) * tile_bytes
            + 4 * chunk_elems * w_up.dtype.itemsize
            + (4 * chunk_elems * mm_bytes if emit_weights else 0)
            + TM * D * mm_bytes + TM * FC * mm_bytes + 3 * TM * slab * 4 + tile_bytes // 2 + 2 * MIB)

    hn_a_scratch = [pltpu.VMEM((TM, D), MATMUL_DTYPE), pltpu.VMEM((TM, FC), MATMUL_DTYPE)]
    n_chunks = F // FC
    if not emit_weights and n_chunks % 2 == 0:
        vec_spec = pl.BlockSpec((1, D), lambda i: (0, 0))
        tile_spec = pl.BlockSpec((TM, D), lambda i: (i, 0))
        return pl.pallas_call(
            functools.partial(_mlp_tile_kernel, n_chunks=n_chunks),
            grid=(N // TM,),
            in_specs=[tile_spec, vec_spec, pl.BlockSpec(memory_space=pl.ANY),
                      pl.BlockSpec(memory_space=pl.ANY), vec_spec],
            out_specs=tile_spec,
            out_shape=jax.ShapeDtypeStruct((N, D), h.dtype),
            scratch_shapes=hn_a_scratch + [
                pltpu.VMEM((2, D, FC), MATMUL_DTYPE), pltpu.VMEM((2, FC, D), MATMUL_DTYPE),
                pltpu.SemaphoreType.DMA((2, 2))],
            compiler_params=pltpu.CompilerParams(
                dimension_semantics=("arbitrary",), vmem_limit_bytes=int(vmem)),
            name="mlp_tiles",
        )(h, g2, w_up, w_down, gf)

    h_spec = pl.BlockSpec((TM, D), lambda i, j: (i, 0),
                          pipeline_mode=pl.Buffered(1) if single_tile else None)
    y_spec = pl.BlockSpec((TM, D), lambda i, j: (i, 0))
    y_shape = jax.ShapeDtypeStruct((N, D), h.dtype)
    if emit_weights:
        out_specs = [y_spec, pl.BlockSpec((D, FC), lambda i, j: (0, j)),
                     pl.BlockSpec((FC, D), lambda i, j: (j, 0))]
        out_shape = [y_shape, jax.ShapeDtypeStruct(w_up.shape, MATMUL_DTYPE),
                     jax.ShapeDtypeStruct(w_down.shape, MATMUL_DTYPE)]
    else:
        out_specs, out_shape = y_spec, y_shape

    return pl.pallas_call(
        functools.partial(_mlp_kernel, emit_weights=emit_weights, n_chunks=n_chunks),
        grid=(N // TM, n_chunks),
        in_specs=[
            h_spec,
            pl.BlockSpec((1, D), lambda i, j: (0, 0)),
            pl.BlockSpec((D, FC), lambda i, j: (0, j)),
            pl.BlockSpec((FC, D), lambda i, j: (j, 0)),
            pl.BlockSpec((1, D), lambda i, j: (0, 0)),
        ],
        out_specs=out_specs,
        out_shape=out_shape,
        scratch_shapes=hn_a_scratch,
        compiler_params=pltpu.CompilerParams(
            dimension_semantics=("arbitrary", "arbitrary"), vmem_limit_bytes=int(vmem)),
        name="mlp",
    )(h, g2, w_up, w_down, gf)


def kernel(x_prompt, x_sample, cache_pool, cache_conv, meta_tokens, norm1_g, w_in, w_pool,
           pool_scale, conv_w, w_out, norm2_g, w_up, w_down, final_g):
    depth = norm1_g.shape[0]
    assert depth == 1, "history hand-off from the meta tokens is written for a single layer"
    bp, seq, d_model = x_prompt.shape
    bs, dec_seq, _ = x_sample.shape
    n_meta = meta_tokens.shape[0]
    pool_width = cache_pool.shape[-1]
    conv_width = cache_conv.shape[-1]
    assert n_meta >= POOL_HIST, "prompt rows must see only full pooling windows"
    bf16 = MATMUL_DTYPE
    f32 = jnp.float32

    mixer_weights = (norm1_g[0][None], w_in[0].astype(bf16), w_pool[0].astype(bf16),
                     pool_scale[0][None], conv_w[0], w_out[0].astype(bf16))

    _, meta_pool, meta_conv = _mixer_call(
        meta_tokens.astype(x_prompt.dtype)[None],
        jnp.zeros((POOL_HIST, 1, pool_width), f32), jnp.zeros((1, 1, CONV_HIST, conv_width), f32),
        *mixer_weights, seqs_per_block=1, rows_per_tile=n_meta)

    hp, sp_pool, sp_conv, w_up_mm, w_down_mm = _mixer_call(
        x_prompt, meta_pool, meta_conv, *mixer_weights,
        seqs_per_block=1, rows_per_tile=MIXER_ROWS, side_round=((w_up[0], 1), (w_down[0], 0)))
    hs, ss_pool, ss_conv = _mixer_call(
        x_sample, jnp.transpose(cache_pool[0], (1, 0, 2)), cache_conv, *mixer_weights,
        seqs_per_block=MIXER_ROWS // dec_seq, rows_per_tile=dec_seq)

    g2, gf = norm2_g[0][None], final_g[None]
    y_sample = _mlp_call(hs, g2, w_up_mm, w_down_mm, gf,
                         rows_per_tile=MLP_ROWS, ff_chunk=MLP_FF_CHUNK)
    y_prompt = _mlp_call(hp, g2, w_up_mm, w_down_mm, gf,
                         rows_per_tile=MLP_ROWS, ff_chunk=MLP_FF_CHUNK)
    y_prompt = y_prompt.reshape(bp, seq, d_model)
    y_sample = y_sample.reshape(bs, dec_seq, d_model)

    sp_pool = jnp.transpose(sp_pool, (1, 0, 2))[None]
    ss_pool = jnp.transpose(ss_pool, (1, 0, 2))[None]
    return (y_prompt, y_sample, sp_pool, sp_conv, ss_pool, ss_conv)
```

```python
import functools

import jax
import jax.numpy as jnp
from jax import lax
from jax.experimental import pallas as pl
from jax.experimental.pallas import tpu as pltpu

POOL_WINDOWS = (2, 4, 8, 16)
POOL_HIST = max(POOL_WINDOWS) - 1
CONV_K = 3
CONV_HIST = CONV_K - 1
EPS = 1e-6

SUBLANES = 8
POOL_HEAD = 16
CONV_HEAD = 8
assert POOL_HEAD % SUBLANES == 0 and POOL_HEAD >= POOL_HIST
assert CONV_HEAD % SUBLANES == 0 and CONV_HEAD >= CONV_HIST

MIB = 1024 * 1024

MATMUL_DTYPE = jnp.bfloat16

MIXER_ROWS = 512
MIXER_HEAD_GROUPS = 2
MIXER_HEAD_MIN_ROWS = 128
MLP_ROWS = 512
MLP_FF_CHUNK = 2048
MLP_SLAB = 512
MLP_FF_CHUNK_ROUNDING = 512


def _rmsnorm(x, g):
    r = lax.rsqrt(jnp.mean(x * x, axis=-1, keepdims=True) + EPS)
    return x * r * g


def _dot(a, b):
    return jnp.dot(a, b, preferred_element_type=jnp.float32)


def _mixer_kernel(x_ref, hp_ref, hc_ref, g1_ref, win_ref, wpool_ref, pscale_ref, convw_ref,
                  wout_ref, h_ref, sp_ref, sc_ref, extp_ref, extc_ref, mix_ref, hn_ref):
    S, L, D = x_ref.shape
    P = extp_ref.shape[-1]
    C = extc_ref.shape[-1]
    gc = P // len(POOL_WINDOWS)
    j = pl.program_id(1)
    bf16 = MATMUL_DTYPE

    @pl.when(j == 0)
    def _():
        for r in range(POOL_HIST):
            extp_ref[:, POOL_HEAD - POOL_HIST + r, :] = hp_ref[r]
        extc_ref[:, CONV_HEAD - CONV_HIST:CONV_HEAD, :] = hc_ref[...]

    @pl.when(j > 0)
    def _():
        extp_ref[:, POOL_HEAD - POOL_HIST:POOL_HEAD, :] = extp_ref[:, L + POOL_HEAD - POOL_HIST:L + POOL_HEAD, :]
        extc_ref[:, CONV_HEAD - CONV_HIST:CONV_HEAD, :] = extc_ref[:, L + CONV_HEAD - CONV_HIST:L + CONV_HEAD, :]

    M = S * L
    groups = MIXER_HEAD_GROUPS if M >= MIXER_HEAD_GROUPS * MIXER_HEAD_MIN_ROWS else 1
    for p in range(groups):
        if S > 1:
            sq, rw = slice(p * (S // groups), (p + 1) * (S // groups)), slice(0, L)
        else:
            sq, rw = slice(0, 1), slice(p * (L // groups), (p + 1) * (L // groups))
        rows = slice(p * (M // groups), (p + 1) * (M // groups))
        xp = x_ref[sq, rw, :]
        sp_, lp_ = xp.shape[0], xp.shape[1]
        hn_ref[rows, :] = _rmsnorm(xp.reshape(sp_ * lp_, D), g1_ref[...]).astype(bf16)
        u = _dot(hn_ref[rows, :], win_ref[:, 0:P])
        extp_ref[sq, POOL_HEAD + rw.start:POOL_HEAD + rw.stop, :] = u.reshape(sp_, lp_, P)
    hn = hn_ref[...]

    cg = _dot(hn, win_ref[:, P + C:P + 2 * C])
    v = _dot(hn, win_ref[:, P + 2 * C:P + 3 * C])
    z = (cg * v).reshape(S, L, C)
    extc_ref[:, CONV_HEAD:CONV_HEAD + L, :] = z
    conv = extc_ref[:, CONV_HEAD - 2:CONV_HEAD - 2 + L, :] * convw_ref[0:1, :]
    conv = conv + extc_ref[:, CONV_HEAD - 1:CONV_HEAD - 1 + L, :] * convw_ref[1:2, :]
    conv = conv + z * convw_ref[2:3, :]
    bg = _dot(hn, win_ref[:, P:P + C])
    mix_ref[:, P:P + C] = (bg * conv.reshape(M, C)).astype(bf16)

    for g, w in enumerate(POOL_WINDOWS):
        cols = slice(g * gc, (g + 1) * gc)
        ug = extp_ref[:, POOL_HEAD:POOL_HEAD + L, cols]
        s = ug
        for k in range(1, w):
            s = s + extp_ref[:, POOL_HEAD - k:POOL_HEAD - k + L, cols]
        d = (s * (1.0 / w) - ug).reshape(M, gc).astype(bf16)
        y = _dot(d, wpool_ref[g]) * pscale_ref[:, cols]
        mix_ref[:, cols] = y.astype(bf16)

    h_ref[...] = x_ref[...].reshape(M, D) + _dot(mix_ref[...], wout_ref[...])

    @pl.when(j == pl.num_programs(1) - 1)
    def _():
        for r in range(POOL_HIST):
            sp_ref[r, pl.ds(pl.program_id(0) * S, S), :] = extp_ref[:, L + POOL_HEAD - POOL_HIST + r, :]
        sc_ref[...] = extc_ref[:, L + CONV_HEAD - CONV_HIST:L + CONV_HEAD, :]


def _mixer_call(x, hist_pool, hist_conv, g1, w_in, w_pool, pool_scale, conv_w, w_out, *,
                seqs_per_block, rows_per_tile):
    B, T, D = x.shape
    P = hist_pool.shape[-1]
    C = hist_conv.shape[-1]
    S, L = seqs_per_block, rows_per_tile
    assert B % S == 0 and T % L == 0 and L % SUBLANES == 0 and L >= POOL_HIST
    shared_hist = hist_pool.shape[1] == 1
    assert shared_hist or hist_pool.shape[1] == B
    assert hist_conv.shape[1] == hist_pool.shape[1]
    assert not shared_hist or S == 1

    def hist_pool_map(b, j):
        return (0, 0, 0) if shared_hist else (0, b, 0)

    def hist_conv_map(b, j):
        return (0, 0, 0, 0) if shared_hist else (0, b, 0, 0)

    def resident(a):
        return pl.BlockSpec(a.shape, lambda b, j: (0,) * a.ndim, pipeline_mode=pl.Buffered(1))

    weights = (g1, w_in, w_pool, pool_scale, conv_w, w_out)
    weight_bytes = sum(a.size * a.dtype.itemsize for a in weights)
    tile_bytes = S * L * D * 4
    scratch_bytes = S * ((L + POOL_HEAD) * P + (L + CONV_HEAD) * C) * 4 + S * L * (P + C + D) * 2
    state_bytes = 2 * POOL_HIST * B * P * 4
    vmem = weight_bytes + 4 * tile_bytes + scratch_bytes + state_bytes + 2 * tile_bytes + 2 * MIB

    return pl.pallas_call(
        _mixer_kernel,
        grid=(B // S, T // L),
        in_specs=[
            pl.BlockSpec((S, L, D), lambda b, j: (b, j, 0)),
            pl.BlockSpec((POOL_HIST, S, P), hist_pool_map),
            pl.BlockSpec((None, S, CONV_HIST, C), hist_conv_map),
        ] + [resident(a) for a in weights],
        out_specs=[
            pl.BlockSpec((S * L, D), lambda b, j: (b * (T // L) + j, 0)),
            pl.BlockSpec((POOL_HIST, B, P), lambda b, j: (0, 0, 0)),
            pl.BlockSpec((None, S, CONV_HIST, C), lambda b, j: (0, b, 0, 0)),
        ],
        out_shape=[
            jax.ShapeDtypeStruct((B * T, D), x.dtype),
            jax.ShapeDtypeStruct((POOL_HIST, B, P), x.dtype),
            jax.ShapeDtypeStruct((1, B, CONV_HIST, C), x.dtype),
        ],
        scratch_shapes=[
            pltpu.VMEM((S, POOL_HEAD + L, P), jnp.float32),
            pltpu.VMEM((S, CONV_HEAD + L, C), jnp.float32),
            pltpu.VMEM((S * L, P + C), MATMUL_DTYPE),
            pltpu.VMEM((S * L, D), MATMUL_DTYPE),
        ],
        compiler_params=pltpu.CompilerParams(
            dimension_semantics=("arbitrary", "arbitrary"), vmem_limit_bytes=int(vmem)),
        name="mixer",
    )(x, hist_pool, hist_conv, *weights)


def _mlp_chunk(h_ref, g2_ref, gf_ref, o_ref, hn_ref, a_ref, wu_ref, wd_ref, *, first, last):
    fc = wu_ref.shape[1]
    slab = min(MLP_SLAB, fc)
    if first:
        hn_ref[...] = _rmsnorm(h_ref[...], g2_ref[...]).astype(hn_ref.dtype)
    for k in range(0, fc, slab):
        a = jnp.maximum(_dot(hn_ref[...], wu_ref[:, k:k + slab]), 0.0)
        a_ref[:, k:k + slab] = (a * a).astype(a_ref.dtype)
    base_ref = h_ref if first else o_ref
    acc = base_ref[...] + _dot(a_ref[...], wd_ref[...])
    o_ref[...] = _rmsnorm(acc, gf_ref[...]) if last else acc


def _mlp_tile_kernel(h_ref, g2_ref, wup_hbm, wdown_hbm, gf_ref, o_ref, hn_ref, a_ref,
                     wu_buf, wd_buf, sem, *, n_chunks):
    fc = wu_buf.shape[2]
    i = pl.program_id(0)

    def copies(c, slot):
        return (pltpu.make_async_copy(wup_hbm.at[:, pl.ds(c * fc, fc)], wu_buf.at[slot], sem.at[0, slot]),
                pltpu.make_async_copy(wdown_hbm.at[pl.ds(c * fc, fc), :], wd_buf.at[slot], sem.at[1, slot]))

    def start(c, slot):
        for cp in copies(c, slot):
            cp.start()

    def wait(c, slot):
        for cp in copies(c, slot):
            cp.wait()

    pl.when(i == 0)(lambda: start(0, 0))
    for c in range(n_chunks):
        slot = c % 2
        start((c + 1) % n_chunks, 1 - slot)
        wait(c, slot)
        _mlp_chunk(h_ref, g2_ref, gf_ref, o_ref, hn_ref, a_ref, wu_buf.at[slot], wd_buf.at[slot],
                   first=c == 0, last=c == n_chunks - 1)
    pl.when(i == pl.num_programs(0) - 1)(lambda: wait(0, 0))


def _mlp_kernel(h_ref, g2_ref, wup_ref, wdown_ref, gf_ref, o_ref, *rest, emit_weights, n_chunks):
    hn_ref, a_ref = rest[-2:]
    j = pl.program_id(1)

    def step(first, last):
        wu_ref, wd_ref = wup_ref, wdown_ref
        if emit_weights:
            wu_ref, wd_ref = rest[:2]
            wu_ref[...] = wup_ref[...].astype(wu_ref.dtype)
            wd_ref[...] = wdown_ref[...].astype(wd_ref.dtype)
        _mlp_chunk(h_ref, g2_ref, gf_ref, o_ref, hn_ref, a_ref, wu_ref, wd_ref,
                   first=first, last=last)

    if n_chunks == 1:
        step(True, True)
    else:
        pl.when(j == 0)(lambda: step(True, False))
        pl.when(j == n_chunks - 1)(lambda: step(False, True))
        if n_chunks > 2:
            pl.when(jnp.logical_and(j > 0, j < n_chunks - 1))(lambda: step(False, False))


def _mlp_call(h, g2, w_up, w_down, gf, *, rows_per_tile, ff_chunk):
    N, D = h.shape
    F = w_up.shape[1]
    TM, FC = rows_per_tile, ff_chunk
    slab = min(MLP_SLAB, FC)
    assert N % TM == 0 and F % FC == 0 and FC % slab == 0
    emit_weights = w_up.dtype != MATMUL_DTYPE
    assert w_down.dtype == w_up.dtype
    assert not emit_weights or N == TM
    single_tile = N == TM
    mm_bytes = jnp.dtype(MATMUL_DTYPE).itemsize
    tile_bytes = TM * D * 4
    chunk_elems = D * FC
    vmem = ((3 if single_tile else 4) * tile_bytes
            + 4 * chunk_elems * w_up.dtype.itemsize
            + (4 * chunk_elems * mm_bytes if emit_weights else 0)
            + TM * D * mm_bytes + TM * FC * mm_bytes + 3 * TM * slab * 4 + tile_bytes // 2 + 2 * MIB)

    hn_a_scratch = [pltpu.VMEM((TM, D), MATMUL_DTYPE), pltpu.VMEM((TM, FC), MATMUL_DTYPE)]
    n_chunks = F // FC
    if not emit_weights and n_chunks % 2 == 0:
        vec_spec = pl.BlockSpec((1, D), lambda i: (0, 0))
        tile_spec = pl.BlockSpec((TM, D), lambda i: (i, 0))
        return pl.pallas_call(
            functools.partial(_mlp_tile_kernel, n_chunks=n_chunks),
            grid=(N // TM,),
            in_specs=[tile_spec, vec_spec, pl.BlockSpec(memory_space=pl.ANY),
                      pl.BlockSpec(memory_space=pl.ANY), vec_spec],
            out_specs=tile_spec,
            out_shape=jax.ShapeDtypeStruct((N, D), h.dtype),
            scratch_shapes=hn_a_scratch + [
                pltpu.VMEM((2, D, FC), MATMUL_DTYPE), pltpu.VMEM((2, FC, D), MATMUL_DTYPE),
                pltpu.SemaphoreType.DMA((2, 2))],
            compiler_params=pltpu.CompilerParams(
                dimension_semantics=("arbitrary",), vmem_limit_bytes=int(vmem)),
            name="mlp_tiles",
        )(h, g2, w_up, w_down, gf)

    h_spec = pl.BlockSpec((TM, D), lambda i, j: (i, 0),
                          pipeline_mode=pl.Buffered(1) if single_tile else None)
    y_spec = pl.BlockSpec((TM, D), lambda i, j: (i, 0))
    y_shape = jax.ShapeDtypeStruct((N, D), h.dtype)
    if emit_weights:
        out_specs = [y_spec, pl.BlockSpec((D, FC), lambda i, j: (0, j)),
                     pl.BlockSpec((FC, D), lambda i, j: (j, 0))]
        out_shape = [y_shape, jax.ShapeDtypeStruct(w_up.shape, MATMUL_DTYPE),
                     jax.ShapeDtypeStruct(w_down.shape, MATMUL_DTYPE)]
    else:
        out_specs, out_shape = y_spec, y_shape

    return pl.pallas_call(
        functools.partial(_mlp_kernel, emit_weights=emit_weights, n_chunks=n_chunks),
        grid=(N // TM, n_chunks),
        in_specs=[
            h_spec,
            pl.BlockSpec((1, D), lambda i, j: (0, 0)),
            pl.BlockSpec((D, FC), lambda i, j: (0, j)),
            pl.BlockSpec((FC, D), lambda i, j: (j, 0)),
            pl.BlockSpec((1, D), lambda i, j: (0, 0)),
        ],
        out_specs=out_specs,
        out_shape=out_shape,
        scratch_shapes=hn_a_scratch,
        compiler_params=pltpu.CompilerParams(
            dimension_semantics=("arbitrary", "arbitrary"), vmem_limit_bytes=int(vmem)),
        name="mlp",
    )(h, g2, w_up, w_down, gf)


def kernel(x_prompt, x_sample, cache_pool, cache_conv, meta_tokens, norm1_g, w_in, w_pool,
           pool_scale, conv_w, w_out, norm2_g, w_up, w_down, final_g):
    depth = norm1_g.shape[0]
    assert depth == 1, "history hand-off from the meta tokens is written for a single layer"
    bp, seq, d_model = x_prompt.shape
    bs, dec_seq, _ = x_sample.shape
    n_meta = meta_tokens.shape[0]
    pool_width = cache_pool.shape[-1]
    conv_width = cache_conv.shape[-1]
    assert n_meta >= POOL_HIST, "prompt rows must see only full pooling windows"
    bf16 = MATMUL_DTYPE
    f32 = jnp.float32

    mixer_weights = (norm1_g[0][None], w_in[0].astype(bf16), w_pool[0].astype(bf16),
                     pool_scale[0][None], conv_w[0], w_out[0].astype(bf16))

    _, meta_pool, meta_conv = _mixer_call(
        meta_tokens.astype(x_prompt.dtype)[None],
        jnp.zeros((POOL_HIST, 1, pool_width), f32), jnp.zeros((1, 1, CONV_HIST, conv_width), f32),
        *mixer_weights, seqs_per_block=1, rows_per_tile=n_meta)

    hp, sp_pool, sp_conv = _mixer_call(
        x_prompt, meta_pool, meta_conv, *mixer_weights,
        seqs_per_block=1, rows_per_tile=MIXER_ROWS)
    hs, ss_pool, ss_conv = _mixer_call(
        x_sample, jnp.transpose(cache_pool[0], (1, 0, 2)), cache_conv, *mixer_weights,
        seqs_per_block=MIXER_ROWS // dec_seq, rows_per_tile=dec_seq)

    g2, gf = norm2_g[0][None], final_g[None]
    y_sample, w_up_mm, w_down_mm = _mlp_call(
        hs, g2, w_up[0], w_down[0], gf,
        rows_per_tile=bs * dec_seq, ff_chunk=MLP_FF_CHUNK_ROUNDING)
    y_prompt = _mlp_call(hp, g2, w_up_mm, w_down_mm, gf,
                         rows_per_tile=MLP_ROWS, ff_chunk=MLP_FF_CHUNK)
    y_prompt = y_prompt.reshape(bp, seq, d_model)
    y_sample = y_sample.reshape(bs, dec_seq, d_model)

    sp_pool = jnp.transpose(sp_pool, (1, 0, 2))[None]
    ss_pool = jnp.transpose(ss_pool, (1, 0, 2))[None]
    return (y_prompt, y_sample, sp_pool, sp_conv, ss_pool, ss_conv)
```
